```python
import math
import jax
import jax.numpy as jnp
from jax import lax
import numpy as np

D_MODEL = 1024
BATCH = 16
SEQ = 2048
DEPTH = 2
DEC_BATCH = 32
DEC_SEQ = 1
PAST_LEN = 16384
PAGE_SIZE = 128

HEAD_DIM = 64
H_DIFF = 4
H_MOBA = 4
H_FOX = 4
H_STICK = 4
SLOT_BOUNDS = ((0, 2 * H_DIFF),
               (2 * H_DIFF, 2 * H_DIFF + H_MOBA),
               (2 * H_DIFF + H_MOBA, 2 * H_DIFF + H_MOBA + H_FOX),
               (2 * H_DIFF + H_MOBA + H_FOX, 2 * H_DIFF + H_MOBA + H_FOX + H_STICK))
N_SLOTS = SLOT_BOUNDS[-1][1]
ROPE_SLOTS = SLOT_BOUNDS[1][1]
QKV_WIDTH = N_SLOTS * HEAD_DIM
IN_WIDTH = 3 * QKV_WIDTH + H_FOX
N_BRANCH = len(SLOT_BOUNDS)
MOBA_BLOCK = 256
MOBA_TOPK = 3
MOBA_Q_CHUNK = 16
ATTN_BLOCK = 128
ROPE_THETA = 10000.0
X_HEADS = 4
X_WIDTH = X_HEADS * HEAD_DIM
N_MEM = 256
D_FF = ((8 * D_MODEL // 3 + 127) // 128) * 128
EPS = 1e-6

kernel_name = 'hybrid_gated_branch_decoder_step'


def rms_norm(x, g):
    x32 = x.astype(jnp.float32)
    y = x32 * lax.rsqrt(jnp.mean(x32 * x32, axis=-1, keepdims=True) + EPS)
    return (y * g.astype(jnp.float32)).astype(x.dtype)


def swiglu(x, w_up, w_down):
    a, b = jnp.split(x @ w_up, 2, axis=-1)
    return (jax.nn.silu(a) * b) @ w_down


def rope(x, pos):
    half = x.shape[-1] // 2
    inv = ROPE_THETA ** (-jnp.arange(half, dtype=jnp.float32) / half)
    ang = pos.astype(jnp.float32)[:, None] * inv[None, :]
    cos = jnp.cos(ang)[None, :, None, :]
    sin = jnp.sin(ang)[None, :, None, :]
    x32 = x.astype(jnp.float32)
    x1, x2 = x32[..., :half], x32[..., half:]
    return jnp.concatenate([x1 * cos - x2 * sin, x2 * cos + x1 * sin], axis=-1).astype(x.dtype)


def sweep_queries(fn, block, q_pos, *q_arrays):
    sq = q_pos.shape[0]
    if sq <= block or sq % block:
        return fn(q_pos, *q_arrays)
    nb = sq // block
    split = lambda a: jnp.moveaxis(a.reshape(a.shape[0], nb, block, *a.shape[2:]), 1, 0)
    out = lax.map(lambda xs: fn(*xs), (q_pos.reshape(nb, block),) + tuple(split(a) for a in q_arrays))
    out = jnp.moveaxis(out, 0, 1)
    return out.reshape(out.shape[0], sq, *out.shape[3:])


def diff_attention(qp, q, k, v, lam):
    kpos = jnp.arange(k.shape[1])
    s = jnp.einsum('bqhmd,bkhmd->bmhqk', q, k) * (q.shape[-1] ** -0.5)
    s = jnp.where(kpos[None, :] <= qp[:, None], s, -jnp.inf)
    p = jax.nn.softmax(s, axis=-1)
    a = p[:, 0] - lam * p[:, 1]
    return jnp.einsum('bhqk,bkhe->bqhe', a, v)


def moba_attention(qp, q, kblk, vblk, means):
    bsz, qc, heads, dim = q.shape
    own = qp // MOBA_BLOCK
    own_idx = jnp.broadcast_to(own[None, None, :, None], (bsz, heads, qc, 1))
    n_full = means.shape[2]
    k_sel = min(MOBA_TOPK, n_full)
    if k_sel > 0:
        gate = jnp.einsum('bqhd,bhnd->bhqn', q, means)
        gate = jnp.where(jnp.arange(n_full)[None, :] < own[:, None], gate, -jnp.inf)
        top_val, top_idx = lax.top_k(gate, k_sel)
        idx = jnp.concatenate([top_idx.astype(own_idx.dtype), own_idx], axis=-1)
        valid = jnp.concatenate([jnp.isfinite(top_val), jnp.ones(own_idx.shape, bool)], axis=-1)
    else:
        idx = own_idx
        valid = jnp.ones(own_idx.shape, bool)
    n_sel = idx.shape[-1]
    flat = idx.reshape(bsz, heads, qc * n_sel)
    b_ix = jnp.arange(bsz)[:, None, None]
    h_ix = jnp.arange(heads)[None, :, None]
    kg = kblk[b_ix, h_ix, flat].reshape(bsz, heads, qc, n_sel, MOBA_BLOCK, dim)
    vg = vblk[b_ix, h_ix, flat].reshape(bsz, heads, qc, n_sel, MOBA_BLOCK, dim)
    kpos = idx[..., None] * MOBA_BLOCK + jnp.arange(MOBA_BLOCK)
    mask = valid[..., None] & (kpos <= qp[None, None, :, None, None])
    s = jnp.einsum('bqhd,bhqnsd->bhqns', q, kg) * (dim ** -0.5)
    s = jnp.where(mask, s, -jnp.inf).reshape(bsz, heads, qc, n_sel * MOBA_BLOCK)
    p = jax.nn.softmax(s, axis=-1).reshape(bsz, heads, qc, n_sel, MOBA_BLOCK)
    return jnp.einsum('bhqns,bhqnse->bqhe', p, vg)


def fox_attention(qp, q, fq, k, v, f_cum):
    kpos = jnp.arange(k.shape[1])
    s = jnp.einsum('bqhd,bkhd->bhqk', q, k) * (q.shape[-1] ** -0.5)
    s = s + jnp.swapaxes(fq, 1, 2)[..., None] - jnp.swapaxes(f_cum, 1, 2)[:, :, None, :]
    s = jnp.where(kpos[None, :] <= qp[:, None], s, -jnp.inf)
    p = jax.nn.softmax(s, axis=-1)
    return jnp.einsum('bhqk,bkhe->bqhe', p, v)


def stick_breaking_attention(qp, q, k, v):
    kpos = jnp.arange(k.shape[1])
    z = jnp.einsum('bqhd,bkhd->bhqk', q, k) * (q.shape[-1] ** -0.5)
    strict = kpos[None, :] < qp[:, None]
    log_rest = jnp.where(strict, jax.nn.log_sigmoid(-z), 0.0)
    after = lax.cumsum(log_rest, axis=3, reverse=True) - log_rest
    a = jnp.where(strict, jnp.exp(jax.nn.log_sigmoid(z) + after), 0.0)
    return jnp.einsum('bhqk,bkhe->bqhe', a, v)


def token_mixing(n, q, k_all, v_all, f_all, q_pos, layer_idx, diff_lambda, diff_subln,
                 w_branch, w_gate, b_gate, w_out):
    f32 = jnp.float32
    bsz, sq = q.shape[0], q.shape[1]
    kv_len = k_all.shape[1]
    q, k_all, v_all = q.astype(f32), k_all.astype(f32), v_all.astype(f32)

    def group(a, b):
        lo, hi = SLOT_BOUNDS[b]
        return a[:, :, lo:hi]

    lam_init = 0.8 - 0.6 * math.exp(-0.3 * layer_idx)
    lv = diff_lambda.astype(f32)
    lam = jnp.exp(jnp.sum(lv[0] * lv[1])) - jnp.exp(jnp.sum(lv[2] * lv[3])) + lam_init
    qa = group(q, 0).reshape(bsz, sq, H_DIFF, 2, HEAD_DIM)
    ka = group(k_all, 0).reshape(bsz, kv_len, H_DIFF, 2, HEAD_DIM)
    va = group(v_all, 0).reshape(bsz, kv_len, H_DIFF, 2 * HEAD_DIM)
    o_a = sweep_queries(lambda qp, qb: diff_attention(qp, qb, ka, va, lam), ATTN_BLOCK, q_pos, qa)
    o_a = rms_norm(o_a, diff_subln) * (1.0 - lam_init)

    kb, vb = group(k_all, 1), group(v_all, 1)
    n_full = kv_len // MOBA_BLOCK
    n_blk = -(-kv_len // MOBA_BLOCK)
    pad = n_blk * MOBA_BLOCK - kv_len

    def to_blocks(a):
        a = jnp.pad(a, ((0, 0), (0, pad), (0, 0), (0, 0)))
        return a.reshape(bsz, n_blk, MOBA_BLOCK, H_MOBA, HEAD_DIM).transpose(0, 3, 1, 2, 4)

    kblk, vblk = to_blocks(kb), to_blocks(vb)
    means = jnp.mean(kblk[:, :, :n_full], axis=3)
    o_b = sweep_queries(lambda qp, qc: moba_attention(qp, qc, kblk, vblk, means),
                        MOBA_Q_CHUNK, q_pos, group(q, 1))

    kc, vc = group(k_all, 2), group(v_all, 2)
    f_cum = jnp.cumsum(f_all.astype(f32), axis=1)
    o_c = sweep_queries(lambda qp, qc, fq: fox_attention(qp, qc, fq, kc, vc, f_cum),
                        ATTN_BLOCK, q_pos, group(q, 2), f_cum[:, q_pos])

    kd, vd = group(k_all, 3), group(v_all, 3)
    o_d = sweep_queries(lambda qp, qd: stick_breaking_attention(qp, qd, kd, vd),
                        ATTN_BLOCK, q_pos, group(q, 3))

    outs = (o_a, o_b, o_c, o_d)
    u = jnp.stack([o.reshape(bsz, sq, -1).astype(n.dtype) @ w_branch[lo * HEAD_DIM:hi * HEAD_DIM]
                   for o, (lo, hi) in zip(outs, SLOT_BOUNDS)], axis=2)
    gates = jax.nn.sigmoid(n @ w_gate + b_gate).reshape(bsz, sq, N_BRANCH, D_MODEL)
    return jnp.sum(gates * u, axis=2) @ w_out


def cross_attention(n, mem_k, mem_v, w_xq, w_xo):
    bsz, sq, _ = n.shape
    q = (n @ w_xq).reshape(bsz, sq, X_HEADS, HEAD_DIM).astype(jnp.float32)
    s = jnp.einsum('bqhd,bmhd->bhqm', q, mem_k.astype(jnp.float32)) * (HEAD_DIM ** -0.5)
    p = jax.nn.softmax(s, axis=-1)
    o = jnp.einsum('bhqm,bmhd->bqhd', p, mem_v.astype(jnp.float32)).reshape(bsz, sq, X_WIDTH)
    return o.astype(n.dtype) @ w_xo


def decoder_layer(h, q_pos, past, mem_k, mem_v, layer_idx, ffn1_norm, ffn1_w_up, ffn1_w_down,
                  mix_norm, w_in, b_forget, diff_lambda, diff_subln, w_branch, w_gate, b_gate,
                  w_out, xattn_norm, w_xq, w_xo, ffn2_norm, ffn2_w_up, ffn2_w_down):
    bsz, sq, _ = h.shape
    h = h + 0.5 * swiglu(rms_norm(h, ffn1_norm), ffn1_w_up, ffn1_w_down)
    n = rms_norm(h, mix_norm)
    proj = n @ w_in
    slots = (bsz, sq, N_SLOTS, HEAD_DIM)
    q = proj[..., :QKV_WIDTH].reshape(slots)
    k = proj[..., QKV_WIDTH:2 * QKV_WIDTH].reshape(slots)
    v = proj[..., 2 * QKV_WIDTH:3 * QKV_WIDTH].reshape(slots)
    logf = jax.nn.log_sigmoid(proj[..., 3 * QKV_WIDTH:].astype(jnp.float32)
                              + b_forget.astype(jnp.float32))
    q = jnp.concatenate([rope(q[:, :, :ROPE_SLOTS], q_pos), q[:, :, ROPE_SLOTS:]], axis=2)
    k = jnp.concatenate([rope(k[:, :, :ROPE_SLOTS], q_pos), k[:, :, ROPE_SLOTS:]], axis=2)
    if past is None:
        k_all, v_all, f_all = k, v, logf
    else:
        k_past, v_past, f_past = past
        k_all = jnp.concatenate([k_past.astype(k.dtype), k], axis=1)
        v_all = jnp.concatenate([v_past.astype(v.dtype), v], axis=1)
        f_all = jnp.concatenate([f_past.astype(jnp.float32), logf], axis=1)
    h = h + token_mixing(n, q, k_all, v_all, f_all, q_pos, layer_idx, diff_lambda, diff_subln,
                         w_branch, w_gate, b_gate, w_out)
    h = h + cross_attention(rms_norm(h, xattn_norm), mem_k, mem_v, w_xq, w_xo)
    h = h + 0.5 * swiglu(rms_norm(h, ffn2_norm), ffn2_w_up, ffn2_w_down)
    return h, k, v, logf.astype(h.dtype)


def gather_pages(cache, layer, page_table):
    g = cache[layer, page_table]
    return g.reshape(g.shape[0], g.shape[1] * g.shape[2], *g.shape[3:])


def setup_inputs(seed: int = 0) -> dict:
    key = jax.random.key(seed)
    keys = iter(jax.random.split(key, 48))
    f32 = jnp.float32

    def normal(shape, scale=1.0):
        return scale * jax.random.normal(next(keys), shape, f32)

    def gain(shape):
        return 1.0 + 0.01 * normal(shape)

    n_pages = PAST_LEN // PAGE_SIZE
    n_pool = (DEC_BATCH * n_pages * 5) // 4
    page_rows = (DEPTH, n_pool, PAGE_SIZE)
    page_table = jax.random.permutation(next(keys), n_pool)[: DEC_BATCH * n_pages]
    page_table = page_table.reshape(DEC_BATCH, n_pages).astype(jnp.int32)
    return {
        'x_prompt': normal((BATCH, SEQ, D_MODEL)),
        'x_sample': normal((DEC_BATCH, DEC_SEQ, D_MODEL)),
        'cache_k': normal(page_rows + (N_SLOTS, HEAD_DIM)),
        'cache_v': normal(page_rows + (N_SLOTS, HEAD_DIM)),
        'cache_logf': jax.nn.log_sigmoid(normal(page_rows + (H_FOX,))),
        'cache_mem_k': normal((DEPTH, DEC_BATCH, N_MEM, X_HEADS, HEAD_DIM)),
        'cache_mem_v': normal((DEPTH, DEC_BATCH, N_MEM, X_HEADS, HEAD_DIM)),
        'page_table': page_table,
        'mem_prompt': normal((BATCH, N_MEM, D_MODEL)),
        'ffn1_norm': gain((DEPTH, D_MODEL)),
        'ffn1_w_up': normal((DEPTH, D_MODEL, 2 * D_FF), D_MODEL ** -0.5),
        'ffn1_w_down': normal((DEPTH, D_FF, D_MODEL), D_FF ** -0.5),
        'mix_norm': gain((DEPTH, D_MODEL)),
        'w_in': normal((DEPTH, D_MODEL, IN_WIDTH), D_MODEL ** -0.5),
        'b_forget': normal((DEPTH, H_FOX), 0.1),
        'diff_lambda': normal((DEPTH, 4, HEAD_DIM), 0.1),
        'diff_subln': gain((DEPTH, 2 * HEAD_DIM)),
        'w_branch': normal((DEPTH, QKV_WIDTH, D_MODEL), (QKV_WIDTH // N_BRANCH) ** -0.5),
        'w_gate': normal((DEPTH, D_MODEL, N_BRANCH * D_MODEL), D_MODEL ** -0.5),
        'b_gate': normal((DEPTH, N_BRANCH * D_MODEL), 0.01),
        'w_out': normal((DEPTH, D_MODEL, D_MODEL), D_MODEL ** -0.5),
        'xattn_norm': gain((DEPTH, D_MODEL)),
        'w_xq': normal((DEPTH, D_MODEL, X_WIDTH), D_MODEL ** -0.5),
        'w_xk': normal((DEPTH, D_MODEL, X_WIDTH), D_MODEL ** -0.5),
        'w_xv': normal((DEPTH, D_MODEL, X_WIDTH), D_MODEL ** -0.5),
        'w_xo': normal((DEPTH, X_WIDTH, D_MODEL), X_WIDTH ** -0.5),
        'ffn2_norm': gain((DEPTH, D_MODEL)),
        'ffn2_w_up': normal((DEPTH, D_MODEL, 2 * D_FF), D_MODEL ** -0.5),
        'ffn2_w_down': normal((DEPTH, D_FF, D_MODEL), D_FF ** -0.5),
        'final_norm': gain((D_MODEL,)),
    }


def reference(x_prompt, x_sample, cache_k, cache_v, cache_logf, cache_mem_k, cache_mem_v,
              page_table, mem_prompt, ffn1_norm, ffn1_w_up, ffn1_w_down, mix_norm, w_in,
              b_forget, diff_lambda, diff_subln, w_branch, w_gate, b_gate, w_out, xattn_norm,
              w_xq, w_xk, w_xv, w_xo, ffn2_norm, ffn2_w_up, ffn2_w_down, final_norm):
    past_len = page_table.shape[1] * PAGE_SIZE
    pos_prompt = jnp.arange(x_prompt.shape[1], dtype=jnp.int32)
    pos_sample = past_len + jnp.arange(x_sample.shape[1], dtype=jnp.int32)
    b_p, n_mem = mem_prompt.shape[0], mem_prompt.shape[1]
    h_p, h_s = x_prompt, x_sample
    k_p, v_p, f_p, mk_p, mv_p, k_s, v_s, f_s = [], [], [], [], [], [], [], []
    for l in range(DEPTH):
        layer_w = (ffn1_norm[l], ffn1_w_up[l], ffn1_w_down[l], mix_norm[l], w_in[l], b_forget[l],
                   diff_lambda[l], diff_subln[l], w_branch[l], w_gate[l], b_gate[l], w_out[l],
                   xattn_norm[l], w_xq[l], w_xo[l], ffn2_norm[l], ffn2_w_up[l], ffn2_w_down[l])
        mem_k = (mem_prompt @ w_xk[l]).reshape(b_p, n_mem, X_HEADS, HEAD_DIM)
        mem_v = (mem_prompt @ w_xv[l]).reshape(b_p, n_mem, X_HEADS, HEAD_DIM)
        h_p, k_new, v_new, f_new = decoder_layer(h_p, pos_prompt, None, mem_k, mem_v, l, *layer_w)
        k_p.append(k_new)
        v_p.append(v_new)
        f_p.append(f_new)
        mk_p.append(mem_k)
        mv_p.append(mem_v)
        past = (gather_pages(cache_k, l, page_table), gather_pages(cache_v, l, page_table),
                gather_pages(cache_logf, l, page_table))
        h_s, k_new, v_new, f_new = decoder_layer(h_s, pos_sample, past, cache_mem_k[l],
                                                 cache_mem_v[l], l, *layer_w)
        k_s.append(k_new)
        v_s.append(v_new)
        f_s.append(f_new)
    y_prompt = rms_norm(h_p, final_norm)
    y_sample = rms_norm(h_s, final_norm)
    return (y_prompt, y_sample, jnp.stack(k_p), jnp.stack(v_p), jnp.stack(f_p), jnp.stack(mk_p),
            jnp.stack(mv_p), jnp.stack(k_s), jnp.stack(v_s), jnp.stack(f_s))
```

```python
import functools
import math

import jax
import jax.numpy as jnp
from jax import lax
from jax.experimental import pallas as pl
from jax.experimental.pallas import tpu as pltpu

F32 = jnp.float32
BF16 = jnp.bfloat16

HEAD_DIM = 64
H_DIFF = 4
H_MOBA = 4
H_FOX = 4
H_STICK = 4
N_SLOTS = 2 * H_DIFF + H_MOBA + H_FOX + H_STICK
QKV_WIDTH = N_SLOTS * HEAD_DIM
ROPE_WIDTH = (2 * H_DIFF + H_MOBA) * HEAD_DIM
DIFF_WIDTH = 2 * H_DIFF * HEAD_DIM
MOBA_COL = DIFF_WIDTH
FOX_COL = MOBA_COL + H_MOBA * HEAD_DIM
STICK_COL = FOX_COL + H_FOX * HEAD_DIM
X_HEADS = 4
X_WIDTH = X_HEADS * HEAD_DIM
MOBA_BLOCK = 256
MOBA_TOPK = 3
ROPE_THETA = 10000.0
EPS = 1e-6
PAGE_SIZE = 128
LANES = 128
SCALE = HEAD_DIM ** -0.5
NEG = -1e30
VMEM_LIMIT = 56 * 1024 * 1024


def _cparams(sem):
    return pltpu.CompilerParams(dimension_semantics=sem, vmem_limit_bytes=VMEM_LIMIT)


def _dot(a, b):
    return jnp.dot(a, b, preferred_element_type=F32)


def _dot_nt(a, b):
    return lax.dot_general(a, b, (((1,), (1,)), ((), ())), preferred_element_type=F32)


def _split2(x):
    hi = x.astype(BF16)
    lo = (x - hi.astype(F32)).astype(BF16)
    return hi, lo


def _split3(x):
    hi = x.astype(BF16)
    r = x - hi.astype(F32)
    mid = r.astype(BF16)
    lo = (r - mid.astype(F32)).astype(BF16)
    return hi, mid, lo


def _dot_01(x, u01):
    a, b, c = _split3(x)
    return _dot(a, u01) + _dot(b, u01) + _dot(c, u01)


def _dot_nt_hi(a, b):
    a1, a2 = _split2(a)
    b1, b2 = _split2(b)
    return _dot_nt(a1, b1) + _dot_nt(a1, b2) + _dot_nt(a2, b1)


def _rms(x, g):
    return x * lax.rsqrt(jnp.mean(x * x, axis=-1, keepdims=True) + EPS) * g


def _sigmoid(x):
    return 1.0 / (1.0 + jnp.exp(-x))


def _neg_softplus(z):
    return -(jnp.maximum(z, 0.0) + jnp.log(1.0 + jnp.exp(-jnp.abs(z))))


def _log_sigmoid(z):
    return jnp.minimum(z, 0.0) - jnp.log(1.0 + jnp.exp(-jnp.abs(z)))


def _row_tile(t, want):
    tm = min(t, want)
    assert t % tm == 0
    return tm


def _ffn_kernel(*refs, n_chunks, final):
    if final:
        h_ref, g_ref, wa_ref, wb_ref, wd_ref, fg_ref, o_ref, n_s, acc_s = refs
    else:
        h_ref, g_ref, wa_ref, wb_ref, wd_ref, o_ref, n_s, acc_s = refs
    c = pl.program_id(1)

    @pl.when(c == 0)
    def _():
        n_s[...] = _rms(h_ref[...], g_ref[...]).astype(BF16)
        acc_s[...] = jnp.zeros_like(acc_s)

    n = n_s[...]
    a = _dot(n, wa_ref[...])
    b = _dot(n, wb_ref[...])
    act = (a * _sigmoid(a) * b).astype(BF16)
    acc_s[...] += _dot(act, wd_ref[...])

    @pl.when(c == n_chunks - 1)
    def _():
        y = h_ref[...] + 0.5 * acc_s[...]
        if final:
            y = _rms(y, fg_ref[...])
        o_ref[...] = y


def _ffn(h, g, w_up, w_down, final_g=None):
    t, d = h.shape
    d_ff = w_down.shape[0]
    tm = _row_tile(t, 512)
    tf = d_ff // 2 if (d_ff // 2) % LANES == 0 else d_ff
    n_chunks = d_ff // tf
    final = final_g is not None
    in_specs = [
        pl.BlockSpec((tm, d), lambda i, c: (i, 0)),
        pl.BlockSpec((1, d), lambda i, c: (0, 0)),
        pl.BlockSpec((d, tf), lambda i, c: (0, c)),
        pl.BlockSpec((d, tf), lambda i, c: (0, n_chunks + c)),
        pl.BlockSpec((tf, d), lambda i, c: (c, 0)),
    ]
    args = [h, g.reshape(1, d), w_up, w_up, w_down]
    if final:
        in_specs.append(pl.BlockSpec((1, d), lambda i, c: (0, 0)))
        args.append(final_g.reshape(1, d))
    return pl.pallas_call(
        functools.partial(_ffn_kernel, n_chunks=n_chunks, final=final),
        grid=(t // tm, n_chunks),
        in_specs=in_specs,
        out_specs=pl.BlockSpec((tm, d), lambda i, c: (i, 0)),
        out_shape=jax.ShapeDtypeStruct((t, d), F32),
        scratch_shapes=[pltpu.VMEM((tm, d), BF16), pltpu.VMEM((tm, d), F32)],
        compiler_params=_cparams(("parallel", "arbitrary")),
        name="ffn",
    )(*args)


def _inproj_kernel(h_ref, g_ref, w_ref, wf_ref, bf_ref, cos_ref, sa_ref, sb_ref,
                   qkv_ref, lf_ref, n_s):
    j = pl.program_id(1)

    @pl.when(j == 0)
    def _():
        n = _rms(h_ref[...], g_ref[...]).astype(BF16)
        n_s[...] = n
        lf_ref[...] = _log_sigmoid(_dot(n, wf_ref[...]) + bf_ref[...])

    y = _dot(n_s[...], w_ref[...])

    @pl.when(j < 2)
    def _():
        x = y[:, :ROPE_WIDTH]
        r = (x * cos_ref[...]
             + pltpu.roll(x, ROPE_WIDTH - HEAD_DIM // 2, 1) * sa_ref[...]
             + pltpu.roll(x, HEAD_DIM // 2, 1) * sb_ref[...])
        qkv_ref[0, :, :ROPE_WIDTH] = r
        qkv_ref[0, :, ROPE_WIDTH:] = y[:, ROPE_WIDTH:]

    @pl.when(j == 2)
    def _():
        qkv_ref[0] = y


def _in_proj(h, g, w_qkv, w_f, b_f, cos_t, sin_a, sin_b):
    t, d = h.shape
    tm = _row_tile(t, 512)
    n_pos = cos_t.shape[0] // tm
    assert cos_t.shape[0] % tm == 0
    pos_map = lambda i, j: (i % n_pos, 0)
    return pl.pallas_call(
        _inproj_kernel,
        grid=(t // tm, 3),
        in_specs=[
            pl.BlockSpec((tm, d), lambda i, j: (i, 0)),
            pl.BlockSpec((1, d), lambda i, j: (0, 0)),
            pl.BlockSpec((d, QKV_WIDTH), lambda i, j: (0, j)),
            pl.BlockSpec((d, LANES), lambda i, j: (0, 0)),
            pl.BlockSpec((1, LANES), lambda i, j: (0, 0)),
            pl.BlockSpec((tm, ROPE_WIDTH), pos_map),
            pl.BlockSpec((tm, ROPE_WIDTH), pos_map),
            pl.BlockSpec((tm, ROPE_WIDTH), pos_map),
        ],
        out_specs=[
            pl.BlockSpec((1, tm, QKV_WIDTH), lambda i, j: (j, i, 0)),
            pl.BlockSpec((tm, LANES), lambda i, j: (i, 0)),
        ],
        out_shape=[jax.ShapeDtypeStruct((3, t, QKV_WIDTH), F32),
                   jax.ShapeDtypeStruct((t, LANES), F32)],
        scratch_shapes=[pltpu.VMEM((tm, d), BF16)],
        compiler_params=_cparams(("parallel", "arbitrary")),
        name="in_proj",
    )(h, g.reshape(1, d), w_qkv, w_f, b_f, cos_t, sin_a, sin_b)


def _matmul_kernel(x_ref, w_ref, o_ref):
    o_ref[...] = _dot(x_ref[...].astype(BF16), w_ref[...])


def _matmul(x, w):
    t, d = x.shape
    n = w.shape[1]
    tm = _row_tile(t, 512)
    return pl.pallas_call(
        _matmul_kernel,
        grid=(t // tm,),
        in_specs=[pl.BlockSpec((tm, d), lambda i: (i, 0)),
                  pl.BlockSpec((d, n), lambda i: (0, 0))],
        out_specs=pl.BlockSpec((tm, n), lambda i: (i, 0)),
        out_shape=jax.ShapeDtypeStruct((t, n), F32),
        compiler_params=_cparams(("parallel",)),
        name="mem_kv_proj",
    )(x, w)


def _lane_masks(shape):
    lane = lax.broadcasted_iota(jnp.int32, shape, 1)
    return lane < HEAD_DIM, lane >= HEAD_DIM


def _softmax_step(s, m, l, acc, vb, mask=None):
    m_new = jnp.maximum(m, jnp.max(s, axis=1, keepdims=True))
    p = jnp.exp(s - m_new)
    if mask is not None:
        p = jnp.where(mask, p, 0.0)
    alpha = jnp.exp(m - m_new)
    l = alpha * l + jnp.sum(p, axis=1, keepdims=True)
    acc = alpha * acc + _dot(p.astype(BF16), vb)
    return m_new, l, acc


def _causal(i, j, tq, tk):
    row = i * tq + lax.broadcasted_iota(jnp.int32, (tq, tk), 0)
    col = j * tk + lax.broadcasted_iota(jnp.int32, (tq, tk), 1)
    return row, col


def _diff_kernel(lam_ref, sub_ref, q_ref, k_ref, v_ref, o_ref, *, tq, lam_init):
    i = pl.program_id(2)
    q = q_ref[0, 0] * SCALE
    lo, hi = _lane_masks(q.shape)
    q1 = jnp.where(lo, q, 0.0).astype(BF16)
    q2 = jnp.where(hi, q, 0.0).astype(BF16)

    def body(j, carry):
        m1, l1, a1, m2, l2, a2 = carry
        start = pl.multiple_of(j * tq, tq)
        kb = k_ref[0, 0, pl.ds(start, tq), :].astype(BF16)
        vb = v_ref[0, 0, pl.ds(start, tq), :].astype(BF16)
        row, col = _causal(i, j, tq, tq)
        ok = col <= row
        s1 = jnp.where(ok, _dot_nt(q1, kb), NEG)
        s2 = jnp.where(ok, _dot_nt(q2, kb), NEG)
        m1, l1, a1 = _softmax_step(s1, m1, l1, a1, vb)
        m2, l2, a2 = _softmax_step(s2, m2, l2, a2, vb)
        return m1, l1, a1, m2, l2, a2

    col0 = jnp.full((tq, 1), NEG, F32)
    zero1 = jnp.zeros((tq, 1), F32)
    zacc = jnp.zeros((tq, LANES), F32)
    m1, l1, a1, m2, l2, a2 = lax.fori_loop(0, i + 1, body, (col0, zero1, zacc, col0, zero1, zacc))

    lv = lam_ref[...]
    lam = (jnp.exp(jnp.sum(lv[0:1] * lv[1:2], axis=1, keepdims=True))
           - jnp.exp(jnp.sum(lv[2:3] * lv[3:4], axis=1, keepdims=True)) + lam_init)
    o = a1 / l1 - lam * (a2 / l2)
    o = _rms(o, sub_ref[...]) * (1.0 - lam_init)
    o_ref[0] = o.astype(o_ref.dtype)


def _diff_attention(qkv4, lam, subln, layer_idx):
    _, bsz, s, _ = qkv4.shape
    tq = MOBA_BLOCK
    assert s % tq == 0
    lam_init = 0.8 - 0.6 * math.exp(-0.3 * layer_idx)
    return pl.pallas_call(
        functools.partial(_diff_kernel, tq=tq, lam_init=lam_init),
        grid=(bsz, H_DIFF, s // tq),
        in_specs=[
            pl.BlockSpec((4, HEAD_DIM), lambda b, g, i: (0, 0)),
            pl.BlockSpec((1, 2 * HEAD_DIM), lambda b, g, i: (0, 0)),
            pl.BlockSpec((1, 1, tq, LANES), lambda b, g, i: (0, b, i, g)),
            pl.BlockSpec((1, 1, s, LANES), lambda b, g, i: (1, b, 0, g)),
            pl.BlockSpec((1, 1, s, LANES), lambda b, g, i: (2, b, 0, g)),
        ],
        out_specs=pl.BlockSpec((1, tq, LANES), lambda b, g, i: (b, i, g)),
        out_shape=jax.ShapeDtypeStruct((bsz, s, DIFF_WIDTH), BF16),
        compiler_params=_cparams(("parallel", "parallel", "arbitrary")),
        name="diff_attn",
    )(lam, subln.reshape(1, 2 * HEAD_DIM), qkv4, qkv4, qkv4)


def _moba_kernel(q_ref, k_ref, v_ref, o_ref, kmean_s, *, tq, n_full, k_sel):
    i = pl.program_id(2)

    @pl.when(i == 0)
    def _():
        kmean_s[...] = jnp.zeros_like(kmean_s)
        for n in range(n_full):
            blk = k_ref[0, 0, n * MOBA_BLOCK:(n + 1) * MOBA_BLOCK, :]
            kmean_s[n:n + 1, :] = jnp.sum(blk, axis=0, keepdims=True) * (1.0 / MOBA_BLOCK)

    q = q_ref[0, 0] * SCALE
    lo, hi = _lane_masks(q.shape)
    lane = lax.broadcasted_iota(jnp.int32, (tq, LANES), 1)
    kmean = kmean_s[...]
    heads = []
    for hmask in (lo, hi):
        qh = jnp.where(hmask, q, 0.0)
        gate = _dot_nt_hi(qh, kmean)
        gate = jnp.where(lane < i, gate, -jnp.inf)
        rank = jnp.zeros((tq, LANES), F32)
        for m in range(n_full):
            gm = gate[:, m:m + 1]
            beats = (gm > gate) | ((gm == gate) & (lane > m))
            rank = rank + jnp.where(beats, 1.0, 0.0)
        sel = jnp.where((lane < i) & (rank < k_sel), 1.0, 0.0)
        heads.append((qh.astype(BF16), sel))

    def body(j, carry):
        start = pl.multiple_of(j * tq, tq)
        kb = k_ref[0, 0, pl.ds(start, tq), :].astype(BF16)
        vb = v_ref[0, 0, pl.ds(start, tq), :].astype(BF16)
        _, col = _causal(i, j, tq, tq)
        row1 = i * tq + lax.broadcasted_iota(jnp.int32, (tq, 1), 0)
        out = []
        for (qh, sel), (m, l, a) in zip(heads, carry):
            picked = jnp.max(jnp.where(lane == j, sel, 0.0), axis=1, keepdims=True)
            limit = jnp.where(j == i, row1, jnp.where(picked > 0.5, (j + 1) * tq, -1))
            ok = col <= limit
            s = jnp.where(ok, _dot_nt(qh, kb), NEG)
            out.append(_softmax_step(s, m, l, a, vb, mask=ok))
        return tuple(out)

    init = (jnp.full((tq, 1), NEG, F32), jnp.zeros((tq, 1), F32), jnp.zeros((tq, LANES), F32))
    (m0, l0, a0), (m1, l1, a1) = lax.fori_loop(0, i + 1, body, (init, init))
    o_ref[0] = jnp.where(lo, a0 / l0, a1 / l1).astype(o_ref.dtype)


def _moba_attention(qkv4):
    _, bsz, s, _ = qkv4.shape
    tq = MOBA_BLOCK
    n_full = s // MOBA_BLOCK
    assert s % tq == 0 and n_full <= LANES
    c0 = MOBA_COL // LANES
    return pl.pallas_call(
        functools.partial(_moba_kernel, tq=tq, n_full=n_full, k_sel=min(MOBA_TOPK, n_full)),
        grid=(bsz, H_MOBA // 2, s // tq),
        in_specs=[
            pl.BlockSpec((1, 1, tq, LANES), lambda b, g, i: (0, b, i, c0 + g)),
            pl.BlockSpec((1, 1, s, LANES), lambda b, g, i: (1, b, 0, c0 + g)),
            pl.BlockSpec((1, 1, s, LANES), lambda b, g, i: (2, b, 0, c0 + g)),
        ],
        out_specs=pl.BlockSpec((1, tq, LANES), lambda b, g, i: (b, i, g)),
        out_shape=jax.ShapeDtypeStruct((bsz, s, H_MOBA * HEAD_DIM), BF16),
        scratch_shapes=[pltpu.VMEM((LANES, LANES), F32)],
        compiler_params=_cparams(("parallel", "parallel", "arbitrary")),
        name="moba_attn",
    )(qkv4, qkv4, qkv4)


def _fox_kernel(q_ref, k_ref, v_ref, fq_ref, fk_ref, o_ref, *, tq):
    i = pl.program_id(2)
    q = q_ref[0, 0] * SCALE
    lo, hi = _lane_masks(q.shape)
    qs = (jnp.where(lo, q, 0.0).astype(BF16), jnp.where(hi, q, 0.0).astype(BF16))
    fq = fq_ref[0, 0]

    def body(j, carry):
        start = pl.multiple_of(j * tq, tq)
        kb = k_ref[0, 0, pl.ds(start, tq), :].astype(BF16)
        vb = v_ref[0, 0, pl.ds(start, tq), :].astype(BF16)
        row, col = _causal(i, j, tq, tq)
        ok = col <= row
        out = []
        for hh, (m, l, a) in enumerate(carry):
            fk = fk_ref[0, 0, hh:hh + 1, pl.ds(start, tq)]
            s = _dot_nt(qs[hh], kb) + (fq[:, hh:hh + 1] - fk)
            s = jnp.where(ok, s, NEG)
            out.append(_softmax_step(s, m, l, a, vb))
        return tuple(out)

    init = (jnp.full((tq, 1), NEG, F32), jnp.zeros((tq, 1), F32), jnp.zeros((tq, LANES), F32))
    (m0, l0, a0), (m1, l1, a1) = lax.fori_loop(0, i + 1, body, (init, init))
    o_ref[0] = jnp.where(lo, a0 / l0, a1 / l1).astype(o_ref.dtype)


def _fox_attention(qkv4, f_col, f_row):
    _, bsz, s, _ = qkv4.shape
    tq = MOBA_BLOCK
    c0 = FOX_COL // LANES
    return pl.pallas_call(
        functools.partial(_fox_kernel, tq=tq),
        grid=(bsz, H_FOX // 2, s // tq),
        in_specs=[
            pl.BlockSpec((1, 1, tq, LANES), lambda b, g, i: (0, b, i, c0 + g)),
            pl.BlockSpec((1, 1, s, LANES), lambda b, g, i: (1, b, 0, c0 + g)),
            pl.BlockSpec((1, 1, s, LANES), lambda b, g, i: (2, b, 0, c0 + g)),
            pl.BlockSpec((1, 1, tq, 2), lambda b, g, i: (b, g, i, 0)),
            pl.BlockSpec((1, 1, 2, s), lambda b, g, i: (b, g, 0, 0)),
        ],
        out_specs=pl.BlockSpec((1, tq, LANES), lambda b, g, i: (b, i, g)),
        out_shape=jax.ShapeDtypeStruct((bsz, s, H_FOX * HEAD_DIM), BF16),
        compiler_params=_cparams(("parallel", "parallel", "arbitrary")),
        name="fox_attn",
    )(qkv4, qkv4, qkv4, f_col, f_row)


def _stick_kernel(q_ref, k_ref, v_ref, u_ref, o_ref, *, tq):
    i = pl.program_id(2)
    q = q_ref[0, 0] * SCALE
    lo, hi = _lane_masks(q.shape)
    qs = (jnp.where(lo, q, 0.0).astype(BF16), jnp.where(hi, q, 0.0).astype(BF16))
    u01 = u_ref[...]

    def body(jj, carry):
        j = i - jj
        start = pl.multiple_of(j * tq, tq)
        kb = k_ref[0, 0, pl.ds(start, tq), :].astype(BF16)
        vb = v_ref[0, 0, pl.ds(start, tq), :].astype(BF16)
        row, col = _causal(i, j, tq, tq)
        strict = col < row
        out = []
        for hh, (c, a) in enumerate(carry):
            z = _dot_nt(qs[hh], kb)
            lr = jnp.where(strict, _neg_softplus(z), 0.0)
            after = c + _dot_01(lr, u01)
            w = jnp.where(strict, jnp.exp(z + lr + after), 0.0)
            a = a + _dot(w.astype(BF16), vb)
            c = c + jnp.sum(lr, axis=1, keepdims=True)
            out.append((c, a))
        return tuple(out)

    init = (jnp.zeros((tq, 1), F32), jnp.zeros((tq, LANES), F32))
    (_, a0), (_, a1) = lax.fori_loop(0, i + 1, body, (init, init))
    o_ref[0] = jnp.where(lo, a0, a1).astype(o_ref.dtype)


def _stick_attention(qkv4, u01):
    _, bsz, s, _ = qkv4.shape
    tq = MOBA_BLOCK
    c0 = STICK_COL // LANES
    return pl.pallas_call(
        functools.partial(_stick_kernel, tq=tq),
        grid=(bsz, H_STICK // 2, s // tq),
        in_specs=[
            pl.BlockSpec((1, 1, tq, LANES), lambda b, g, i: (0, b, i, c0 + g)),
            pl.BlockSpec((1, 1, s, LANES), lambda b, g, i: (1, b, 0, c0 + g)),
            pl.BlockSpec((1, 1, s, LANES), lambda b, g, i: (2, b, 0, c0 + g)),
            pl.BlockSpec((tq, tq), lambda b, g, i: (0, 0)),
        ],
        out_specs=pl.BlockSpec((1, tq, LANES), lambda b, g, i: (b, i, g)),
        out_shape=jax.ShapeDtypeStruct((bsz, s, H_STICK * HEAD_DIM), BF16),
        compiler_params=_cparams(("parallel", "parallel", "arbitrary")),
        name="stick_attn",
    )(qkv4, qkv4, qkv4, u01)


def _cumsum_kernel(x_ref, u_ref, o_ref, *, chunk):
    rows, s = x_ref.shape
    carry = jnp.zeros((rows, 1), F32)
    for c in range(s // chunk):
        x = x_ref[:, c * chunk:(c + 1) * chunk]
        o_ref[:, c * chunk:(c + 1) * chunk] = carry + _dot_01(x, u_ref[...])
        carry = carry + jnp.sum(x, axis=1, keepdims=True)


def _cumsum_lanes(x, u_incl):
    rows, s = x.shape
    chunk = u_incl.shape[0]
    assert s % chunk == 0
    return pl.pallas_call(
        functools.partial(_cumsum_kernel, chunk=chunk),
        out_shape=jax.ShapeDtypeStruct((rows, s), F32),
        name="logf_cumsum",
    )(x, u_incl)


def _mix_kernel(h_ref, g_ref, oa_ref, ob_ref, oc_ref, od_ref, wa_ref, wb_ref, wc_ref, wd_ref,
                wg_ref, bg_ref, wo_ref, out_ref):
    h = h_ref[...]
    d = h.shape[1]
    n = _rms(h, g_ref[...]).astype(BF16)
    acc = jnp.zeros(h.shape, F32)
    branches = ((oa_ref, wa_ref), (ob_ref, wb_ref), (oc_ref, wc_ref), (od_ref, wd_ref))
    for b, (o_ref, w_ref) in enumerate(branches):
        gate = _sigmoid(_dot(n, wg_ref[:, b * d:(b + 1) * d]) + bg_ref[:, b * d:(b + 1) * d])
        acc = acc + gate * _dot(o_ref[...], w_ref[...])
    out_ref[...] = h + _dot(acc.astype(BF16), wo_ref[...])


def _mix_out(h, g, outs, w_branch, w_gate, b_gate, w_out):
    t, d = h.shape
    tm = _row_tile(t, 512)
    widths = [o.shape[1] for o in outs]
    offs = [0]
    for w in widths:
        offs.append(offs[-1] + w)
    wbs = [w_branch[offs[b]:offs[b + 1]] for b in range(4)]
    row = lambda i: (i, 0)
    const = lambda i: (0, 0)
    return pl.pallas_call(
        _mix_kernel,
        grid=(t // tm,),
        in_specs=([pl.BlockSpec((tm, d), row), pl.BlockSpec((1, d), const)]
                  + [pl.BlockSpec((tm, w), row) for w in widths]
                  + [pl.BlockSpec((w, d), const) for w in widths]
                  + [pl.BlockSpec((d, 4 * d), const), pl.BlockSpec((1, 4 * d), const),
                     pl.BlockSpec((d, d), const)]),
        out_specs=pl.BlockSpec((tm, d), row),
        out_shape=jax.ShapeDtypeStruct((t, d), F32),
        compiler_params=_cparams(("parallel",)),
        name="mix_out",
    )(h, g.reshape(1, d), *outs, *wbs, w_gate, b_gate.reshape(1, 4 * d), w_out)


def _xattn_kernel(h_ref, g_ref, wq_ref, mk_ref, mv_ref, wo_ref, out_ref):
    rows = h_ref.shape[1]
    h = h_ref[0]
    if rows < 8:
        h = jnp.broadcast_to(h[:1], (8, h.shape[1]))
    n = _rms(h, g_ref[...]).astype(BF16)
    q = _dot(n, wq_ref[...]) * SCALE
    mk = mk_ref[0].astype(BF16)
    mv = mv_ref[0].astype(BF16)
    lane = lax.broadcasted_iota(jnp.int32, q.shape, 1)
    o = jnp.zeros(q.shape, F32)
    for hh in range(X_HEADS):
        own = (lane >= hh * HEAD_DIM) & (lane < (hh + 1) * HEAD_DIM)
        s = _dot_nt(jnp.where(own, q, 0.0).astype(BF16), mk)
        p = jnp.exp(s - jnp.max(s, axis=1, keepdims=True))
        pv = _dot(p.astype(BF16), mv) / jnp.sum(p, axis=1, keepdims=True)
        o = jnp.where(own, pv, o)
    out_ref[0] = (h + _dot(o.astype(BF16), wo_ref[...]))[:rows]


def _cross_attention(h3, g, w_xq, mem_k, mem_v, w_xo):
    bsz, s, d = h3.shape
    n_mem = mem_k.shape[1]
    assert s == 1 or s % 8 == 0
    tm = _row_tile(s, 512)
    return pl.pallas_call(
        _xattn_kernel,
        grid=(bsz, s // tm),
        in_specs=[
            pl.BlockSpec((1, tm, d), lambda b, i: (b, i, 0)),
            pl.BlockSpec((1, d), lambda b, i: (0, 0)),
            pl.BlockSpec((d, X_WIDTH), lambda b, i: (0, 0)),
            pl.BlockSpec((1, n_mem, X_WIDTH), lambda b, i: (b, 0, 0)),
            pl.BlockSpec((1, n_mem, X_WIDTH), lambda b, i: (b, 0, 0)),
            pl.BlockSpec((X_WIDTH, d), lambda b, i: (0, 0)),
        ],
        out_specs=pl.BlockSpec((1, tm, d), lambda b, i: (b, i, 0)),
        out_shape=jax.ShapeDtypeStruct((bsz, s, d), F32),
        compiler_params=_cparams(("parallel", "parallel")),
        name="cross_attn",
    )(h3, g.reshape(1, d), w_xq, mem_k, mem_v, w_xo)


DEC_ROWS = 32


def _dec_row_slot(r):
    return jnp.where(r < 12, r, jnp.where(r < 16, r + 4, jnp.where(r < 20, r - 4, -1)))


def _decode_kernel(pt_ref, lam_ref, sub_ref, q_ref, kn_ref, vn_ref, fn_ref,
                   ka_ref, kb_ref, va_ref, vb_ref, fa_ref, fb_ref, u_ref, o_ref,
                   qrow_s, m_s, l_s, acc_s, cs_s, cf_s, mbm_s, mbl_s, mba0_s, mba1_s, ks0_s, ks1_s,
                   *, nb, k_sel, lam_init):
    del pt_ref
    p = pl.program_id(1)
    n = nb - 1 - p
    r_ = DEC_ROWS
    rowk = lax.broadcasted_iota(jnp.int32, (r_, 1), 0)
    is_sm = (rowk < 8) | ((rowk >= 16) & (rowk < 20))
    is_mb = (rowk >= 8) & (rowk < 12)
    is_st = (rowk >= 12) & (rowk < 16)
    rowi = lax.broadcasted_iota(jnp.int32, (r_, QKV_WIDTH), 0)
    coli = lax.broadcasted_iota(jnp.int32, (r_, QKV_WIDTH), 1)
    own_slot = (coli // HEAD_DIM) == _dec_row_slot(rowi)
    kn = kn_ref[0]
    vn = vn_ref[0]

    @pl.when(p == 0)
    def _():
        qrow = jnp.where(own_slot, q_ref[0] * SCALE, 0.0)
        qrow_s[...] = qrow
        s_self = jnp.sum(qrow * kn, axis=1, keepdims=True)
        m_s[...] = jnp.broadcast_to(s_self, m_s.shape)
        l_s[...] = jnp.ones_like(l_s)
        acc_s[...] = jnp.where(is_sm, jnp.broadcast_to(vn, acc_s.shape), 0.0)
        cs_s[...] = jnp.zeros_like(cs_s)
        sub8 = lax.broadcasted_iota(jnp.int32, (8, LANES), 0)
        lane8 = lax.broadcasted_iota(jnp.int32, (8, LANES), 1)
        f_new = jnp.sum(jnp.where(sub8 == lane8, jnp.broadcast_to(fn_ref[0], (8, LANES)), 0.0),
                        axis=1, keepdims=True)
        cf_s[...] = jnp.broadcast_to(f_new, cf_s.shape)

    qb = qrow_s[...].astype(BF16)
    ka = ka_ref[0, 0]
    kb = kb_ref[0, 0]
    s = jnp.concatenate([_dot_nt(qb, ka.astype(BF16)), _dot_nt(qb, kb.astype(BF16))], axis=1)
    u01 = u_ref[...]

    lf = jnp.concatenate([fa_ref[0, 0], fb_ref[0, 0]], axis=1)
    cf = cf_s[:, :1]
    g_bias = cf + _dot_01(lf, u01)
    cf_s[...] = jnp.broadcast_to(cf + jnp.sum(lf, axis=1, keepdims=True), cf_s.shape)

    z8 = s[8:16]
    sub8w = lax.broadcasted_iota(jnp.int32, z8.shape, 0)
    lr = jnp.where(sub8w >= 4, _neg_softplus(z8), 0.0)
    cs = cs_s[:, :1]
    add8 = jnp.where(sub8w >= 4, lr + cs + _dot_01(lr, u01), 0.0)
    cs_s[...] = jnp.broadcast_to(cs + jnp.sum(lr, axis=1, keepdims=True), cs_s.shape)

    zeros8 = jnp.zeros_like(z8)
    sb = s + jnp.concatenate([zeros8, add8, g_bias, zeros8], axis=0)
    m_blk = jnp.max(sb, axis=1, keepdims=True)
    m_old = m_s[:, :1]
    m_new = jnp.maximum(m_old, m_blk)
    ref = jnp.where(is_sm, m_new, jnp.where(is_mb, m_blk, 0.0))
    pm = jnp.exp(sb - ref)
    alpha = jnp.where(is_sm, jnp.exp(m_old - m_new), jnp.where(is_st, 1.0, 0.0))
    l_blk = jnp.sum(pm, axis=1, keepdims=True)
    l_s[...] = jnp.broadcast_to(alpha * l_s[:, :1] + l_blk, l_s.shape)
    m_s[...] = jnp.broadcast_to(jnp.where(is_sm, m_new, m_old), m_s.shape)
    pb = pm.astype(BF16)
    pv = (_dot(pb[:, :PAGE_SIZE], va_ref[0, 0].astype(BF16))
          + _dot(pb[:, PAGE_SIZE:], vb_ref[0, 0].astype(BF16)))
    acc_s[...] = alpha * acc_s[...] + pv

    base = pl.multiple_of(n * 8, 8)
    mbm_s[pl.ds(base, 8), :] = jnp.broadcast_to(m_blk[8:16], (8, LANES))
    mbl_s[pl.ds(base, 8), :] = jnp.broadcast_to(l_blk[8:16], (8, LANES))
    for half, (acc_half, ksum_half) in enumerate(((mba0_s, ks0_s), (mba1_s, ks1_s))):
        c0 = MOBA_COL + half * LANES
        acc_half[pl.ds(base, 8), :] = pv[8:16, c0:c0 + LANES]
        ksum = (jnp.sum(ka[:, c0:c0 + LANES], axis=0, keepdims=True)
                + jnp.sum(kb[:, c0:c0 + LANES], axis=0, keepdims=True))
        ksum_half[pl.ds(base, 8), :] = jnp.broadcast_to(ksum, (8, LANES))

    @pl.when(p == nb - 1)
    def _():
        lv = lam_ref[...]
        lam = (jnp.exp(jnp.sum(lv[0:1] * lv[1:2], axis=1, keepdims=True))
               - jnp.exp(jnp.sum(lv[2:3] * lv[3:4], axis=1, keepdims=True)) + lam_init)
        inv_l = 1.0 / l_s[:, :1]
        coef = jnp.where(rowk < 8, jnp.where(rowk % 2 == 0, inv_l, -lam * inv_l),
                         jnp.where(is_st, 1.0, jnp.where(is_sm, inv_l, 0.0)))
        pick = (((rowi < 8) & (coli // (2 * HEAD_DIM) == rowi // 2) & (coli < DIFF_WIDTH))
                | (own_slot & (rowi >= 12)))
        dense = jnp.sum(jnp.where(pick, acc_s[...] * coef, 0.0), axis=0, keepdims=True)
        pieces = []
        for hh in range(H_DIFF):
            x = dense[:, hh * 2 * HEAD_DIM:(hh + 1) * 2 * HEAD_DIM]
            pieces.append(_rms(x, sub_ref[...]) * (1.0 - lam_init))

        sub_n = lax.broadcasted_iota(jnp.int32, (nb, LANES), 0)
        lane_n = lax.broadcasted_iota(jnp.int32, (nb, LANES), 1)
        lane_1 = lax.broadcasted_iota(jnp.int32, (1, LANES), 1)
        s_self = jnp.sum(qrow_s[...] * kn, axis=1, keepdims=True)
        moba = []
        for half, (acc_half, ksum_half) in enumerate(((mba0_s, ks0_s), (mba1_s, ks1_s))):
            c0 = MOBA_COL + half * LANES
            prod = (ksum_half[pl.ds(0, nb, stride=8), :] * (1.0 / MOBA_BLOCK)) * q_ref[0][:, c0:c0 + LANES]
            vn_m = vn[:, c0:c0 + LANES]
            o_half = jnp.zeros((1, LANES), F32)
            for sub in range(LANES // HEAD_DIM):
                hh = half * (LANES // HEAD_DIM) + sub
                gate = jnp.sum(jnp.where(lane_n // HEAD_DIM == sub, prod, 0.0), axis=1, keepdims=True)
                g_row = jnp.sum(jnp.where(sub_n == lane_n, jnp.broadcast_to(gate, (nb, LANES)), 0.0),
                                axis=0, keepdims=True)
                g_row = jnp.where(lane_1 < nb, g_row, -jnp.inf)
                beats = (g_row > gate) | ((g_row == gate) & (lane_n < sub_n))
                rank = jnp.sum(jnp.where(beats, 1.0, 0.0), axis=1, keepdims=True)
                sel = rank < k_sel
                m_h = mbm_s[pl.ds(hh, nb, stride=8), :][:, :1]
                l_h = mbl_s[pl.ds(hh, nb, stride=8), :][:, :1]
                a_h = acc_half[pl.ds(hh, nb, stride=8), :]
                ss = s_self[8 + hh:9 + hh]
                top = jnp.maximum(jnp.max(jnp.where(sel, m_h, NEG), axis=0, keepdims=True), ss)
                w = jnp.where(sel, jnp.exp(m_h - top), 0.0)
                w_self = jnp.exp(ss - top)
                num = jnp.sum(w * a_h, axis=0, keepdims=True) + w_self * vn_m
                den = jnp.sum(w * l_h, axis=0, keepdims=True) + w_self
                o_half = jnp.where(lane_1 // HEAD_DIM == sub, num / den, o_half)
            moba.append(o_half)
        out = jnp.concatenate(pieces + moba + [dense[:, FOX_COL:]], axis=1)
        o_ref[0] = out.astype(o_ref.dtype)


def _decode_mixing(qkv, logf_new, cache_k4, cache_v4, cache_ft, page_table, lam, subln, u_excl,
                   layer_idx):
    bsz = qkv.shape[1]
    n_pages = page_table.shape[1]
    assert n_pages % 2 == 0 and 2 * PAGE_SIZE == MOBA_BLOCK
    nb = n_pages // 2
    assert nb <= LANES
    lam_init = 0.8 - 0.6 * math.exp(-0.3 * layer_idx)
    l_ = layer_idx
    page_a = lambda b, p, pt: (l_, pt[b, 2 * (nb - 1 - p)], 0, 0)
    page_b = lambda b, p, pt: (l_, pt[b, 2 * (nb - 1 - p) + 1], 0, 0)
    const = lambda b, p, pt: (0, 0)
    qkv3 = qkv.reshape(3 * bsz, 1, QKV_WIDTH)
    tok = lambda j: (lambda b, p, pt: (j * bsz + b, 0, 0))
    grid_spec = pltpu.PrefetchScalarGridSpec(
        num_scalar_prefetch=1,
        grid=(bsz, nb),
        in_specs=[
            pl.BlockSpec((4, HEAD_DIM), const),
            pl.BlockSpec((1, 2 * HEAD_DIM), const),
            pl.BlockSpec((1, 1, QKV_WIDTH), tok(0)),
            pl.BlockSpec((1, 1, QKV_WIDTH), tok(1)),
            pl.BlockSpec((1, 1, QKV_WIDTH), tok(2)),
            pl.BlockSpec((1, 1, LANES), lambda b, p, pt: (b, 0, 0)),
            pl.BlockSpec((1, 1, PAGE_SIZE, QKV_WIDTH), page_a),
            pl.BlockSpec((1, 1, PAGE_SIZE, QKV_WIDTH), page_b),
            pl.BlockSpec((1, 1, PAGE_SIZE, QKV_WIDTH), page_a),
            pl.BlockSpec((1, 1, PAGE_SIZE, QKV_WIDTH), page_b),
            pl.BlockSpec((1, 1, 8, PAGE_SIZE), page_a),
            pl.BlockSpec((1, 1, 8, PAGE_SIZE), page_b),
            pl.BlockSpec((MOBA_BLOCK, MOBA_BLOCK), const),
        ],
        out_specs=pl.BlockSpec((1, 1, QKV_WIDTH), lambda b, p, pt: (b, 0, 0)),
        scratch_shapes=[
            pltpu.VMEM((DEC_ROWS, QKV_WIDTH), F32),
            pltpu.VMEM((DEC_ROWS, LANES), F32),
            pltpu.VMEM((DEC_ROWS, LANES), F32),
            pltpu.VMEM((DEC_ROWS, QKV_WIDTH), F32),
            pltpu.VMEM((8, LANES), F32),
            pltpu.VMEM((8, LANES), F32),
            pltpu.VMEM((nb * 8, LANES), F32),
            pltpu.VMEM((nb * 8, LANES), F32),
            pltpu.VMEM((nb * 8, LANES), F32),
            pltpu.VMEM((nb * 8, LANES), F32),
            pltpu.VMEM((nb * 8, LANES), F32),
            pltpu.VMEM((nb * 8, LANES), F32),
        ],
    )
    out = pl.pallas_call(
        functools.partial(_decode_kernel, nb=nb, k_sel=min(MOBA_TOPK, nb), lam_init=lam_init),
        grid_spec=grid_spec,
        out_shape=jax.ShapeDtypeStruct((bsz, 1, QKV_WIDTH), BF16),
        compiler_params=_cparams(("parallel", "arbitrary")),
        name="decode_mix",
    )(page_table, lam, subln.reshape(1, 2 * HEAD_DIM), qkv3, qkv3, qkv3,
      logf_new.reshape(bsz, 1, LANES), cache_k4, cache_k4, cache_v4, cache_v4, cache_ft, cache_ft,
      u_excl)
    return out.reshape(bsz, QKV_WIDTH)


def _rope_tables(pos):
    half = HEAD_DIM // 2
    inv = ROPE_THETA ** (-jnp.arange(half, dtype=F32) / half)
    ang = pos.astype(F32)[:, None] * inv[None, :]
    cos, sin = jnp.cos(ang), jnp.sin(ang)
    zero = jnp.zeros_like(sin)
    reps = ROPE_WIDTH // HEAD_DIM
    cos_t = jnp.tile(jnp.concatenate([cos, cos], axis=1), (1, reps))
    sin_a = jnp.tile(jnp.concatenate([-sin, zero], axis=1), (1, reps))
    sin_b = jnp.tile(jnp.concatenate([zero, sin], axis=1), (1, reps))
    return cos_t, sin_a, sin_b


def _tri(n, strict_lower):
    j = jnp.arange(n)[:, None]
    s = jnp.arange(n)[None, :]
    return ((j > s) if strict_lower else (j <= s)).astype(BF16)


def kernel(x_prompt, x_sample, cache_k, cache_v, cache_logf, cache_mem_k, cache_mem_v, page_table,
           mem_prompt, ffn1_norm, ffn1_w_up, ffn1_w_down, mix_norm, w_in, b_forget, diff_lambda,
           diff_subln, w_branch, w_gate, b_gate, w_out, xattn_norm, w_xq, w_xk, w_xv, w_xo,
           ffn2_norm, ffn2_w_up, ffn2_w_down, final_norm):
    bp, sp, d = x_prompt.shape
    bs, ss, _ = x_sample.shape
    assert ss == 1
    depth = ffn1_norm.shape[0]
    n_mem = mem_prompt.shape[1]
    n_pool = cache_k.shape[1]
    past_len = page_table.shape[1] * PAGE_SIZE

    pos_p = jnp.arange(sp, dtype=jnp.int32)
    pos_s = jnp.full((bs,), past_len, dtype=jnp.int32)
    rope_p = _rope_tables(pos_p)
    rope_s = _rope_tables(pos_s)
    u_excl = _tri(MOBA_BLOCK, True)
    u_incl = _tri(MOBA_BLOCK, False)

    cache_k4 = cache_k.reshape(depth, n_pool, PAGE_SIZE, QKV_WIDTH)
    cache_v4 = cache_v.reshape(depth, n_pool, PAGE_SIZE, QKV_WIDTH)
    cache_ft = jnp.pad(jnp.swapaxes(cache_logf, 2, 3), ((0, 0), (0, 0), (0, 8 - H_FOX), (0, 0)))

    h_p = x_prompt.reshape(bp * sp, d)
    h_s = x_sample.reshape(bs, d)
    mem2 = mem_prompt.reshape(bp * n_mem, d)
    outs = {k: [] for k in ("k_p", "v_p", "f_p", "mk_p", "mv_p", "k_s", "v_s", "f_s")}

    for l in range(depth):
        wu1 = ffn1_w_up[l].astype(BF16)
        wd1 = ffn1_w_down[l].astype(BF16)
        wu2 = ffn2_w_up[l].astype(BF16)
        wd2 = ffn2_w_down[l].astype(BF16)
        w_qkv = w_in[l][:, :3 * QKV_WIDTH].astype(BF16)
        w_f = jnp.pad(w_in[l][:, 3 * QKV_WIDTH:], ((0, 0), (0, LANES - H_FOX))).astype(BF16)
        b_f = jnp.pad(b_forget[l], (0, LANES - H_FOX)).reshape(1, LANES)
        wbr = w_branch[l].astype(BF16)
        wg = w_gate[l].astype(BF16)
        wo = w_out[l].astype(BF16)
        wxq = w_xq[l].astype(BF16)
        wxo = w_xo[l].astype(BF16)
        wxkv = jnp.concatenate([w_xk[l], w_xv[l]], axis=1).astype(BF16)
        last = l == depth - 1

        mem_kv = _matmul(mem2, wxkv)
        mem_k = mem_kv[:, :X_WIDTH].reshape(bp, n_mem, X_WIDTH)
        mem_v = mem_kv[:, X_WIDTH:].reshape(bp, n_mem, X_WIDTH)
        outs["mk_p"].append(mem_k.reshape(bp, n_mem, X_HEADS, HEAD_DIM))
        outs["mv_p"].append(mem_v.reshape(bp, n_mem, X_HEADS, HEAD_DIM))

        h_p = _ffn(h_p, ffn1_norm[l], wu1, wd1)
        qkv, lf = _in_proj(h_p, mix_norm[l], w_qkv, w_f, b_f, *rope_p)
        logf = lf[:, :H_FOX].reshape(bp, sp, H_FOX)
        outs["k_p"].append(qkv[1].reshape(bp, sp, N_SLOTS, HEAD_DIM))
        outs["v_p"].append(qkv[2].reshape(bp, sp, N_SLOTS, HEAD_DIM))
        outs["f_p"].append(logf)
        qkv4 = qkv.reshape(3, bp, sp, QKV_WIDTH)
        f_cum = _cumsum_lanes(jnp.swapaxes(logf, 1, 2).reshape(bp * H_FOX, sp), u_incl)
        f_row = f_cum.reshape(bp, H_FOX // 2, 2, sp)
        f_col = jnp.swapaxes(f_row, 2, 3)
        o_a = _diff_attention(qkv4, diff_lambda[l], diff_subln[l], l)
        o_b = _moba_attention(qkv4)
        o_c = _fox_attention(qkv4, f_col, f_row)
        o_d = _stick_attention(qkv4, u_excl)
        flat = lambda o: o.reshape(bp * sp, o.shape[-1])
        h_p = _mix_out(h_p, mix_norm[l], [flat(o_a), flat(o_b), flat(o_c), flat(o_d)],
                       wbr, wg, b_gate[l], wo)
        h_p = _cross_attention(h_p.reshape(bp, sp, d), xattn_norm[l], wxq, mem_k, mem_v,
                               wxo).reshape(bp * sp, d)
        h_p = _ffn(h_p, ffn2_norm[l], wu2, wd2, final_norm if last else None)

        h_s = _ffn(h_s, ffn1_norm[l], wu1, wd1)
        qkv_s, lf_s = _in_proj(h_s, mix_norm[l], w_qkv, w_f, b_f, *rope_s)
        outs["k_s"].append(qkv_s[1].reshape(bs, 1, N_SLOTS, HEAD_DIM))
        outs["v_s"].append(qkv_s[2].reshape(bs, 1, N_SLOTS, HEAD_DIM))
        outs["f_s"].append(lf_s[:, :H_FOX].reshape(bs, 1, H_FOX))
        o_s = _decode_mixing(qkv_s, lf_s, cache_k4, cache_v4, cache_ft, page_table,
                             diff_lambda[l], diff_subln[l], u_excl, l)
        h_s = _mix_out(h_s, mix_norm[l],
                       [o_s[:, :MOBA_COL], o_s[:, MOBA_COL:FOX_COL], o_s[:, FOX_COL:STICK_COL],
                        o_s[:, STICK_COL:]], wbr, wg, b_gate[l], wo)
        h_s = _cross_attention(h_s.reshape(bs, 1, d), xattn_norm[l], wxq,
                               cache_mem_k[l].reshape(bs, n_mem, X_WIDTH),
                               cache_mem_v[l].reshape(bs, n_mem, X_WIDTH), wxo).reshape(bs, d)
        h_s = _ffn(h_s, ffn2_norm[l], wu2, wd2, final_norm if last else None)

    st = jnp.stack
    return (h_p.reshape(bp, sp, d), h_s.reshape(bs, 1, d), st(outs["k_p"]), st(outs["v_p"]),
            st(outs["f_p"]), st(outs["mk_p"]), st(outs["mv_p"]), st(outs["k_s"]), st(outs["v_s"]),
            st(outs["f_s"]))
```

```python
import functools
import math

import jax
import jax.numpy as jnp
from jax import lax
from jax.experimental import pallas as pl
from jax.experimental.pallas import tpu as pltpu

F32 = jnp.float32
BF16 = jnp.bfloat16

HEAD_DIM = 64
H_DIFF = 4
H_MOBA = 4
H_FOX = 4
H_STICK = 4
N_SLOTS = 2 * H_DIFF + H_MOBA + H_FOX + H_STICK
QKV_WIDTH = N_SLOTS * HEAD_DIM
ROPE_WIDTH = (2 * H_DIFF + H_MOBA) * HEAD_DIM
DIFF_WIDTH = 2 * H_DIFF * HEAD_DIM
MOBA_COL = DIFF_WIDTH
FOX_COL = MOBA_COL + H_MOBA * HEAD_DIM
STICK_COL = FOX_COL + H_FOX * HEAD_DIM
X_HEADS = 4
X_WIDTH = X_HEADS * HEAD_DIM
MOBA_BLOCK = 256
MOBA_TOPK = 3
ROPE_THETA = 10000.0
EPS = 1e-6
PAGE_SIZE = 128
LANES = 128
SCALE = HEAD_DIM ** -0.5
NEG = -1e30
VMEM_LIMIT = 56 * 1024 * 1024


def _cparams(sem):
    return pltpu.CompilerParams(dimension_semantics=sem, vmem_limit_bytes=VMEM_LIMIT)


def _dot(a, b):
    return jnp.dot(a, b, preferred_element_type=F32)


def _dot_nt(a, b):
    return lax.dot_general(a, b, (((1,), (1,)), ((), ())), preferred_element_type=F32)


def _split2(x):
    hi = x.astype(BF16)
    lo = (x - hi.astype(F32)).astype(BF16)
    return hi, lo


def _split3(x):
    hi = x.astype(BF16)
    r = x - hi.astype(F32)
    mid = r.astype(BF16)
    lo = (r - mid.astype(F32)).astype(BF16)
    return hi, mid, lo


def _dot_01(x, u01):
    a, b, c = _split3(x)
    return _dot(a, u01) + _dot(b, u01) + _dot(c, u01)


def _dot_nt_hi(a, b):
    a1, a2 = _split2(a)
    b1, b2 = _split2(b)
    return _dot_nt(a1, b1) + _dot_nt(a1, b2) + _dot_nt(a2, b1)


def _rms(x, g):
    return x * lax.rsqrt(jnp.mean(x * x, axis=-1, keepdims=True) + EPS) * g


def _sigmoid(x):
    return 1.0 / (1.0 + jnp.exp(-x))


def _neg_softplus(z):
    return -(jnp.maximum(z, 0.0) + jnp.log(1.0 + jnp.exp(-jnp.abs(z))))


def _log_sigmoid(z):
    return jnp.minimum(z, 0.0) - jnp.log(1.0 + jnp.exp(-jnp.abs(z)))


def _row_tile(t, want):
    tm = min(t, want)
    assert t % tm == 0
    return tm


def _ffn_kernel(*refs, n_chunks, final):
    if final:
        h_ref, g_ref, wa_ref, wb_ref, wd_ref, fg_ref, o_ref, n_s, acc_s = refs
    else:
        h_ref, g_ref, wa_ref, wb_ref, wd_ref, o_ref, n_s, acc_s = refs
    c = pl.program_id(1)

    @pl.when(c == 0)
    def _():
        n_s[...] = _rms(h_ref[...], g_ref[...]).astype(BF16)
        acc_s[...] = jnp.zeros_like(acc_s)

    n = n_s[...]
    a = _dot(n, wa_ref[...])
    b = _dot(n, wb_ref[...])
    act = (a * _sigmoid(a) * b).astype(BF16)
    acc_s[...] += _dot(act, wd_ref[...])

    @pl.when(c == n_chunks - 1)
    def _():
        y = h_ref[...] + 0.5 * acc_s[...]
        if final:
            y = _rms(y, fg_ref[...])
        o_ref[...] = y


def _ffn(h, g, w_up, w_down, final_g=None):
    t, d = h.shape
    d_ff = w_down.shape[0]
    tm = _row_tile(t, 512)
    tf = d_ff // 2 if (d_ff // 2) % LANES == 0 else d_ff
    n_chunks = d_ff // tf
    final = final_g is not None
    in_specs = [
        pl.BlockSpec((tm, d), lambda i, c: (i, 0)),
        pl.BlockSpec((1, d), lambda i, c: (0, 0)),
        pl.BlockSpec((d, tf), lambda i, c: (0, c)),
        pl.BlockSpec((d, tf), lambda i, c: (0, n_chunks + c)),
        pl.BlockSpec((tf, d), lambda i, c: (c, 0)),
    ]
    args = [h, g.reshape(1, d), w_up, w_up, w_down]
    if final:
        in_specs.append(pl.BlockSpec((1, d), lambda i, c: (0, 0)))
        args.append(final_g.reshape(1, d))
    return pl.pallas_call(
        functools.partial(_ffn_kernel, n_chunks=n_chunks, final=final),
        grid=(t // tm, n_chunks),
        in_specs=in_specs,
        out_specs=pl.BlockSpec((tm, d), lambda i, c: (i, 0)),
        out_shape=jax.ShapeDtypeStruct((t, d), F32),
        scratch_shapes=[pltpu.VMEM((tm, d), BF16), pltpu.VMEM((tm, d), F32)],
        compiler_params=_cparams(("parallel", "arbitrary")),
        name="ffn",
    )(*args)


def _inproj_kernel(h_ref, g_ref, w_ref, wf_ref, bf_ref, cos_ref, sa_ref, sb_ref,
                   q_ref, k_ref, v_ref, lf_ref, n_s):
    j = pl.program_id(1)

    @pl.when(j == 0)
    def _():
        n = _rms(h_ref[...], g_ref[...]).astype(BF16)
        n_s[...] = n
        lf_ref[...] = _log_sigmoid(_dot(n, wf_ref[...]) + bf_ref[...])

    y = _dot(n_s[...], w_ref[...])

    def rotary(o_ref):
        x = y[:, :ROPE_WIDTH]
        r = (x * cos_ref[...]
             + pltpu.roll(x, ROPE_WIDTH - HEAD_DIM // 2, 1) * sa_ref[...]
             + pltpu.roll(x, HEAD_DIM // 2, 1) * sb_ref[...])
        o_ref[:, :ROPE_WIDTH] = r
        o_ref[:, ROPE_WIDTH:] = y[:, ROPE_WIDTH:]

    @pl.when(j == 0)
    def _():
        rotary(q_ref)

    @pl.when(j == 1)
    def _():
        rotary(k_ref)

    @pl.when(j == 2)
    def _():
        v_ref[...] = y


def _in_proj(h, g, w_qkv, w_f, b_f, cos_t, sin_a, sin_b):
    t, d = h.shape
    tm = _row_tile(t, 512)
    n_pos = cos_t.shape[0] // tm
    assert cos_t.shape[0] % tm == 0
    pos_map = lambda i, j: (i % n_pos, 0)
    return pl.pallas_call(
        _inproj_kernel,
        grid=(t // tm, 3),
        in_specs=[
            pl.BlockSpec((tm, d), lambda i, j: (i, 0)),
            pl.BlockSpec((1, d), lambda i, j: (0, 0)),
            pl.BlockSpec((d, QKV_WIDTH), lambda i, j: (0, j)),
            pl.BlockSpec((d, LANES), lambda i, j: (0, 0)),
            pl.BlockSpec((1, LANES), lambda i, j: (0, 0)),
            pl.BlockSpec((tm, ROPE_WIDTH), pos_map),
            pl.BlockSpec((tm, ROPE_WIDTH), pos_map),
            pl.BlockSpec((tm, ROPE_WIDTH), pos_map),
        ],
        out_specs=[pl.BlockSpec((tm, QKV_WIDTH), lambda i, j: (i, 0))] * 3
        + [pl.BlockSpec((tm, LANES), lambda i, j: (i, 0))],
        out_shape=[jax.ShapeDtypeStruct((t, QKV_WIDTH), F32)] * 3
        + [jax.ShapeDtypeStruct((t, LANES), F32)],
        scratch_shapes=[pltpu.VMEM((tm, d), BF16)],
        compiler_params=_cparams(("parallel", "arbitrary")),
        name="in_proj",
    )(h, g.reshape(1, d), w_qkv, w_f, b_f, cos_t, sin_a, sin_b)


def _matmul_kernel(x_ref, w_ref, o_ref):
    o_ref[...] = _dot(x_ref[...].astype(BF16), w_ref[...])


def _matmul(x, w):
    t, d = x.shape
    n = w.shape[1]
    tm = _row_tile(t, 512)
    return pl.pallas_call(
        _matmul_kernel,
        grid=(t // tm,),
        in_specs=[pl.BlockSpec((tm, d), lambda i: (i, 0)),
                  pl.BlockSpec((d, n), lambda i: (0, 0))],
        out_specs=pl.BlockSpec((tm, n), lambda i: (i, 0)),
        out_shape=jax.ShapeDtypeStruct((t, n), F32),
        compiler_params=_cparams(("parallel",)),
        name="mem_kv_proj",
    )(x, w)


def _lane_masks(shape):
    lane = lax.broadcasted_iota(jnp.int32, shape, 1)
    return lane < HEAD_DIM, lane >= HEAD_DIM


def _tri_mask(tq, strict):
    row = lax.broadcasted_iota(jnp.int32, (tq, tq), 0)
    col = lax.broadcasted_iota(jnp.int32, (tq, tq), 1)
    return (col < row) if strict else (col <= row)


def _mask_last_block(s, tq, keep):
    last = jnp.where(keep, s[:, -tq:], NEG)
    if s.shape[1] == tq:
        return last
    return jnp.concatenate([s[:, :-tq], last], axis=1)


def _softmax_pv(s, vb):
    p = jnp.exp(s - jnp.max(s, axis=1, keepdims=True))
    return _dot(p.astype(BF16), vb) / jnp.sum(p, axis=1, keepdims=True)


def _for_each_query_block(nq, fn):
    i = pl.program_id(2)
    for n in range(nq):
        pl.when(i == n)(functools.partial(fn, n))


def _diff_kernel(lam_ref, sub_ref, q_ref, k_ref, v_ref, o_ref, *, tq, nq, lam_init):
    def branch(n):
        length = (n + 1) * tq
        q = q_ref[0] * SCALE
        lo, hi = _lane_masks(q.shape)
        kb = k_ref[0, :length, :].astype(BF16)
        vb = v_ref[0, :length, :].astype(BF16)
        keep = _tri_mask(tq, False)
        o1 = _softmax_pv(_mask_last_block(_dot_nt(jnp.where(lo, q, 0.0).astype(BF16), kb), tq, keep), vb)
        o2 = _softmax_pv(_mask_last_block(_dot_nt(jnp.where(hi, q, 0.0).astype(BF16), kb), tq, keep), vb)
        lv = lam_ref[...]
        lam = (jnp.exp(jnp.sum(lv[0:1] * lv[1:2], axis=1, keepdims=True))
               - jnp.exp(jnp.sum(lv[2:3] * lv[3:4], axis=1, keepdims=True)) + lam_init)
        o = _rms(o1 - lam * o2, sub_ref[...]) * (1.0 - lam_init)
        o_ref[0] = o.astype(o_ref.dtype)

    _for_each_query_block(nq, branch)


def _diff_attention(q3, k3, v3, lam, subln, layer_idx):
    bsz, s, _ = q3.shape
    tq = MOBA_BLOCK
    assert s % tq == 0
    lam_init = 0.8 - 0.6 * math.exp(-0.3 * layer_idx)
    return pl.pallas_call(
        functools.partial(_diff_kernel, tq=tq, nq=s // tq, lam_init=lam_init),
        grid=(bsz, H_DIFF, s // tq),
        in_specs=[
            pl.BlockSpec((4, HEAD_DIM), lambda b, g, i: (0, 0)),
            pl.BlockSpec((1, 2 * HEAD_DIM), lambda b, g, i: (0, 0)),
            pl.BlockSpec((1, tq, LANES), lambda b, g, i: (b, i, g)),
            pl.BlockSpec((1, s, LANES), lambda b, g, i: (b, 0, g)),
            pl.BlockSpec((1, s, LANES), lambda b, g, i: (b, 0, g)),
        ],
        out_specs=pl.BlockSpec((1, tq, LANES), lambda b, g, i: (b, i, g)),
        out_shape=jax.ShapeDtypeStruct((bsz, s, DIFF_WIDTH), BF16),
        compiler_params=_cparams(("parallel", "parallel", "arbitrary")),
        name="diff_attn",
    )(lam, subln.reshape(1, 2 * HEAD_DIM), q3, k3, v3)


def _moba_kernel(q_ref, k_ref, v_ref, o_ref, kmean_s, *, tq, nq, k_sel):
    i = pl.program_id(2)

    @pl.when(i == 0)
    def _():
        kmean_s[...] = jnp.zeros_like(kmean_s)
        for n in range(nq):
            blk = k_ref[0, n * MOBA_BLOCK:(n + 1) * MOBA_BLOCK, :]
            kmean_s[n:n + 1, :] = jnp.sum(blk, axis=0, keepdims=True) * (1.0 / MOBA_BLOCK)

    def branch(n):
        length = (n + 1) * tq
        q = q_ref[0] * SCALE
        lane = lax.broadcasted_iota(jnp.int32, (tq, LANES), 1)
        kb = k_ref[0, :length, :].astype(BF16)
        vb = v_ref[0, :length, :].astype(BF16)
        keep = _tri_mask(tq, False)
        kmean = kmean_s[...]
        outs = []
        for hmask in _lane_masks(q.shape):
            qh = jnp.where(hmask, q, 0.0)
            s = _dot_nt(qh.astype(BF16), kb)
            pieces = []
            if n > 0:
                gate = jnp.where(lane < n, _dot_nt_hi(qh, kmean), -jnp.inf)
                rank = jnp.zeros((tq, LANES), F32)
                for m in range(n):
                    gm = gate[:, m:m + 1]
                    beats = (gm > gate) | ((gm == gate) & (lane > m))
                    rank = rank + jnp.where(beats, 1.0, 0.0)
                for m in range(n):
                    hidden = jnp.where(rank[:, m:m + 1] < k_sel, 0.0, NEG)
                    pieces.append(s[:, m * tq:(m + 1) * tq] + hidden)
            pieces.append(jnp.where(keep, s[:, n * tq:], NEG))
            s = pieces[0] if n == 0 else jnp.concatenate(pieces, axis=1)
            outs.append(_softmax_pv(s, vb))
        o_ref[0] = jnp.where(lane < HEAD_DIM, outs[0], outs[1]).astype(o_ref.dtype)

    _for_each_query_block(nq, branch)


def _moba_attention(q3, k3, v3):
    bsz, s, _ = q3.shape
    tq = MOBA_BLOCK
    n_full = s // MOBA_BLOCK
    assert s % tq == 0 and n_full <= LANES
    c0 = MOBA_COL // LANES
    return pl.pallas_call(
        functools.partial(_moba_kernel, tq=tq, nq=n_full, k_sel=min(MOBA_TOPK, n_full)),
        grid=(bsz, H_MOBA // 2, s // tq),
        in_specs=[
            pl.BlockSpec((1, tq, LANES), lambda b, g, i: (b, i, c0 + g)),
            pl.BlockSpec((1, s, LANES), lambda b, g, i: (b, 0, c0 + g)),
            pl.BlockSpec((1, s, LANES), lambda b, g, i: (b, 0, c0 + g)),
        ],
        out_specs=pl.BlockSpec((1, tq, LANES), lambda b, g, i: (b, i, g)),
        out_shape=jax.ShapeDtypeStruct((bsz, s, H_MOBA * HEAD_DIM), BF16),
        scratch_shapes=[pltpu.VMEM((LANES, LANES), F32)],
        compiler_params=_cparams(("parallel", "parallel", "arbitrary")),
        name="moba_attn",
    )(q3, k3, v3)


def _fox_kernel(q_ref, k_ref, v_ref, fq_ref, fk_ref, o_ref, *, tq, nq):
    def branch(n):
        length = (n + 1) * tq
        q = q_ref[0] * SCALE
        lo, hi = _lane_masks(q.shape)
        kb = k_ref[0, :length, :].astype(BF16)
        vb = v_ref[0, :length, :].astype(BF16)
        keep = _tri_mask(tq, False)
        fq = fq_ref[0, 0]
        outs = []
        for hh, hmask in enumerate((lo, hi)):
            fk = fk_ref[0, 0, hh:hh + 1, :length]
            s = _dot_nt(jnp.where(hmask, q, 0.0).astype(BF16), kb) + (fq[:, hh:hh + 1] - fk)
            outs.append(_softmax_pv(_mask_last_block(s, tq, keep), vb))
        o_ref[0] = jnp.where(lo, outs[0], outs[1]).astype(o_ref.dtype)

    _for_each_query_block(nq, branch)


def _fox_attention(q3, k3, v3, f_col, f_row):
    bsz, s, _ = q3.shape
    tq = MOBA_BLOCK
    c0 = FOX_COL // LANES
    return pl.pallas_call(
        functools.partial(_fox_kernel, tq=tq, nq=s // tq),
        grid=(bsz, H_FOX // 2, s // tq),
        in_specs=[
            pl.BlockSpec((1, tq, LANES), lambda b, g, i: (b, i, c0 + g)),
            pl.BlockSpec((1, s, LANES), lambda b, g, i: (b, 0, c0 + g)),
            pl.BlockSpec((1, s, LANES), lambda b, g, i: (b, 0, c0 + g)),
            pl.BlockSpec((1, 1, tq, 2), lambda b, g, i: (b, g, i, 0)),
            pl.BlockSpec((1, 1, 2, s), lambda b, g, i: (b, g, 0, 0)),
        ],
        out_specs=pl.BlockSpec((1, tq, LANES), lambda b, g, i: (b, i, g)),
        out_shape=jax.ShapeDtypeStruct((bsz, s, H_FOX * HEAD_DIM), BF16),
        compiler_params=_cparams(("parallel", "parallel", "arbitrary")),
        name="fox_attn",
    )(q3, k3, v3, f_col, f_row)


def _stick_kernel(q_ref, k_ref, v_ref, u_ref, o_ref, *, tq, nq):
    def branch(n):
        length = (n + 1) * tq
        q = q_ref[0] * SCALE
        lo, hi = _lane_masks(q.shape)
        kb = k_ref[0, :length, :].astype(BF16)
        vb = v_ref[0, :length, :].astype(BF16)
        strict = _tri_mask(tq, True)
        u01 = u_ref[...]
        outs = []
        for hmask in (lo, hi):
            z = _dot_nt(jnp.where(hmask, q, 0.0).astype(BF16), kb)
            carry = jnp.zeros((tq, 1), F32)
            weights = [None] * (n + 1)
            for m in range(n, -1, -1):
                zm = z[:, m * tq:(m + 1) * tq]
                lr = _neg_softplus(zm)
                if m == n:
                    lr = jnp.where(strict, lr, 0.0)
                e = zm + lr + carry + _dot_01(lr, u01)
                if m == n:
                    e = jnp.where(strict, e, NEG)
                weights[m] = jnp.exp(e).astype(BF16)
                carry = carry + jnp.sum(lr, axis=1, keepdims=True)
            w = weights[0] if n == 0 else jnp.concatenate(weights, axis=1)
            outs.append(_dot(w, vb))
        o_ref[0] = jnp.where(lo, outs[0], outs[1]).astype(o_ref.dtype)

    _for_each_query_block(nq, branch)


def _stick_attention(q3, k3, v3, u01):
    bsz, s, _ = q3.shape
    tq = MOBA_BLOCK
    c0 = STICK_COL // LANES
    return pl.pallas_call(
        functools.partial(_stick_kernel, tq=tq, nq=s // tq),
        grid=(bsz, H_STICK // 2, s // tq),
        in_specs=[
            pl.BlockSpec((1, tq, LANES), lambda b, g, i: (b, i, c0 + g)),
            pl.BlockSpec((1, s, LANES), lambda b, g, i: (b, 0, c0 + g)),
            pl.BlockSpec((1, s, LANES), lambda b, g, i: (b, 0, c0 + g)),
            pl.BlockSpec((tq, tq), lambda b, g, i: (0, 0)),
        ],
        out_specs=pl.BlockSpec((1, tq, LANES), lambda b, g, i: (b, i, g)),
        out_shape=jax.ShapeDtypeStruct((bsz, s, H_STICK * HEAD_DIM), BF16),
        compiler_params=_cparams(("parallel", "parallel", "arbitrary")),
        name="stick_attn",
    )(q3, k3, v3, u01)


def _cumsum_kernel(x_ref, u_ref, o_ref, *, chunk):
    rows, s = x_ref.shape
    carry = jnp.zeros((rows, 1), F32)
    for c in range(s // chunk):
        x = x_ref[:, c * chunk:(c + 1) * chunk]
        o_ref[:, c * chunk:(c + 1) * chunk] = carry + _dot_01(x, u_ref[...])
        carry = carry + jnp.sum(x, axis=1, keepdims=True)


def _cumsum_lanes(x, u_incl):
    rows, s = x.shape
    chunk = u_incl.shape[0]
    assert s % chunk == 0
    return pl.pallas_call(
        functools.partial(_cumsum_kernel, chunk=chunk),
        out_shape=jax.ShapeDtypeStruct((rows, s), F32),
        name="logf_cumsum",
    )(x, u_incl)


def _mix_kernel(h_ref, g_ref, oa_ref, ob_ref, oc_ref, od_ref, wa_ref, wb_ref, wc_ref, wd_ref,
                wg_ref, bg_ref, wo_ref, out_ref):
    h = h_ref[...]
    d = h.shape[1]
    n = _rms(h, g_ref[...]).astype(BF16)
    acc = jnp.zeros(h.shape, F32)
    branches = ((oa_ref, wa_ref), (ob_ref, wb_ref), (oc_ref, wc_ref), (od_ref, wd_ref))
    for b, (o_ref, w_ref) in enumerate(branches):
        gate = _sigmoid(_dot(n, wg_ref[:, b * d:(b + 1) * d]) + bg_ref[:, b * d:(b + 1) * d])
        acc = acc + gate * _dot(o_ref[...].astype(BF16), w_ref[...])
    out_ref[...] = h + _dot(acc.astype(BF16), wo_ref[...])


def _mix_out(h, g, outs, w_branch, w_gate, b_gate, w_out):
    t, d = h.shape
    tm = _row_tile(t, 512)
    widths = [o.shape[1] for o in outs]
    offs = [0]
    for w in widths:
        offs.append(offs[-1] + w)
    wbs = [w_branch[offs[b]:offs[b + 1]] for b in range(4)]
    row = lambda i: (i, 0)
    const = lambda i: (0, 0)
    return pl.pallas_call(
        _mix_kernel,
        grid=(t // tm,),
        in_specs=([pl.BlockSpec((tm, d), row), pl.BlockSpec((1, d), const)]
                  + [pl.BlockSpec((tm, w), row) for w in widths]
                  + [pl.BlockSpec((w, d), const) for w in widths]
                  + [pl.BlockSpec((d, 4 * d), const), pl.BlockSpec((1, 4 * d), const),
                     pl.BlockSpec((d, d), const)]),
        out_specs=pl.BlockSpec((tm, d), row),
        out_shape=jax.ShapeDtypeStruct((t, d), F32),
        compiler_params=_cparams(("parallel",)),
        name="mix_out",
    )(h, g.reshape(1, d), *outs, *wbs, w_gate, b_gate.reshape(1, 4 * d), w_out)


def _xattn_kernel(h_ref, g_ref, wq_ref, mk_ref, mv_ref, wo_ref, out_ref):
    rows = h_ref.shape[1]
    h = h_ref[0]
    if rows < 8:
        h = jnp.broadcast_to(h[:1], (8, h.shape[1]))
    n = _rms(h, g_ref[...]).astype(BF16)
    q = _dot(n, wq_ref[...]) * SCALE
    mk = mk_ref[0].astype(BF16)
    mv = mv_ref[0].astype(BF16)
    lane = lax.broadcasted_iota(jnp.int32, q.shape, 1)
    o = jnp.zeros(q.shape, F32)
    for hh in range(X_HEADS):
        own = (lane >= hh * HEAD_DIM) & (lane < (hh + 1) * HEAD_DIM)
        s = _dot_nt(jnp.where(own, q, 0.0).astype(BF16), mk)
        p = jnp.exp(s - jnp.max(s, axis=1, keepdims=True))
        pv = _dot(p.astype(BF16), mv) / jnp.sum(p, axis=1, keepdims=True)
        o = jnp.where(own, pv, o)
    out_ref[0] = (h + _dot(o.astype(BF16), wo_ref[...]))[:rows]


def _cross_attention(h3, g, w_xq, mem_k, mem_v, w_xo):
    bsz, s, d = h3.shape
    n_mem = mem_k.shape[1]
    assert s == 1 or s % 8 == 0
    tm = _row_tile(s, 512)
    return pl.pallas_call(
        _xattn_kernel,
        grid=(bsz, s // tm),
        in_specs=[
            pl.BlockSpec((1, tm, d), lambda b, i: (b, i, 0)),
            pl.BlockSpec((1, d), lambda b, i: (0, 0)),
            pl.BlockSpec((d, X_WIDTH), lambda b, i: (0, 0)),
            pl.BlockSpec((1, n_mem, X_WIDTH), lambda b, i: (b, 0, 0)),
            pl.BlockSpec((1, n_mem, X_WIDTH), lambda b, i: (b, 0, 0)),
            pl.BlockSpec((X_WIDTH, d), lambda b, i: (0, 0)),
        ],
        out_specs=pl.BlockSpec((1, tm, d), lambda b, i: (b, i, 0)),
        out_shape=jax.ShapeDtypeStruct((bsz, s, d), F32),
        compiler_params=_cparams(("parallel", "parallel")),
        name="cross_attn",
    )(h3, g.reshape(1, d), w_xq, mem_k, mem_v, w_xo)


DEC_ROWS = 32


def _dec_row_slot(r):
    return jnp.where(r < 12, r, jnp.where(r < 16, r + 4, jnp.where(r < 20, r - 4, -1)))


def _col_to_row(col, width):
    n = col.shape[0]
    sub = lax.broadcasted_iota(jnp.int32, (n, width), 0)
    lane = lax.broadcasted_iota(jnp.int32, (n, width), 1)
    return jnp.sum(jnp.where(sub == lane, jnp.broadcast_to(col, (n, width)), 0.0), axis=0, keepdims=True)


def _row_to_col(row):
    n = row.shape[1]
    sub = lax.broadcasted_iota(jnp.int32, (n, n), 0)
    lane = lax.broadcasted_iota(jnp.int32, (n, n), 1)
    return jnp.sum(jnp.where(sub == lane, jnp.broadcast_to(row, (n, n)), 0.0), axis=1, keepdims=True)


def _decode_kernel(pt_ref, lam_ref, subc_ref, q_ref, qc_ref, kn_ref, vnc_ref, fn_ref,
                   ka_ref, kb_ref, va_ref, vb_ref, fa_ref, fb_ref, u_ref, o_ref,
                   qrow_s, m_s, l_s, acc_s, cs_s, cf_s, gate_s, mbm_s, mbl_s, mba_s,
                   *, nb, k_sel, lam_init):
    del pt_ref
    p = pl.program_id(1)
    n = nb - 1 - p
    r_ = DEC_ROWS
    rowk = lax.broadcasted_iota(jnp.int32, (r_, 1), 0)
    is_sm = (rowk < 8) | ((rowk >= 16) & (rowk < 20))
    is_mb = (rowk >= 8) & (rowk < 12)
    is_st = (rowk >= 12) & (rowk < 16)
    lane1 = lax.broadcasted_iota(jnp.int32, (1, LANES), 1)
    sm_lane = (lane1 < 8) | ((lane1 >= 16) & (lane1 < 20))
    st_lane = (lane1 >= 12) & (lane1 < 16)
    lane8 = lax.broadcasted_iota(jnp.int32, (8, LANES), 1)
    sub8 = lax.broadcasted_iota(jnp.int32, (8, LANES), 0)
    kn = kn_ref[0]
    vnc = vnc_ref[0]
    mw = H_MOBA * HEAD_DIM

    @pl.when(p == 0)
    def _():
        rowi = lax.broadcasted_iota(jnp.int32, (r_, QKV_WIDTH), 0)
        coli = lax.broadcasted_iota(jnp.int32, (r_, QKV_WIDTH), 1)
        qrow = jnp.where((coli // HEAD_DIM) == _dec_row_slot(rowi), q_ref[0] * SCALE, 0.0)
        qrow_s[...] = qrow
        s_self = jnp.sum(qrow * kn, axis=1, keepdims=True)
        m_s[...] = jnp.broadcast_to(s_self, m_s.shape)
        l_s[...] = jnp.ones_like(l_s)
        acc_s[...] = jnp.where(sm_lane, jnp.broadcast_to(vnc, acc_s.shape), 0.0)
        cs_s[...] = jnp.zeros_like(cs_s)
        f_new = jnp.sum(jnp.where(sub8 == lane8, jnp.broadcast_to(fn_ref[0], (8, LANES)), 0.0),
                        axis=1, keepdims=True)
        cf_s[...] = jnp.broadcast_to(f_new, cf_s.shape)
        gate_s[...] = jnp.zeros_like(gate_s)
        mbm_s[...] = jnp.zeros_like(mbm_s)
        mbl_s[...] = jnp.zeros_like(mbl_s)
        mba_s[...] = jnp.zeros_like(mba_s)

    qb = qrow_s[...].astype(BF16)
    kta = ka_ref[0, 0]
    ktb = kb_ref[0, 0]
    s = jnp.concatenate([_dot(qb, kta.astype(BF16)), _dot(qb, ktb.astype(BF16))], axis=1)
    u01 = u_ref[...]

    lf = jnp.concatenate([fa_ref[0, 0], fb_ref[0, 0]], axis=1)
    cf = cf_s[:, :1]
    g_bias = cf + _dot_01(lf, u01)
    cf_s[...] = jnp.broadcast_to(cf + jnp.sum(lf, axis=1, keepdims=True), cf_s.shape)

    z8 = s[8:16]
    sub8w = lax.broadcasted_iota(jnp.int32, z8.shape, 0)
    lr = jnp.where(sub8w >= 4, _neg_softplus(z8), 0.0)
    cs = cs_s[:, :1]
    add8 = jnp.where(sub8w >= 4, lr + cs + _dot_01(lr, u01), 0.0)
    cs_s[...] = jnp.broadcast_to(cs + jnp.sum(lr, axis=1, keepdims=True), cs_s.shape)

    zeros8 = jnp.zeros_like(z8)
    sb = s + jnp.concatenate([zeros8, add8, g_bias, zeros8], axis=0)
    m_blk = jnp.max(sb, axis=1, keepdims=True)
    m_old = m_s[:, :1]
    m_new = jnp.maximum(m_old, m_blk)
    ref = jnp.where(is_sm, m_new, jnp.where(is_mb, m_blk, 0.0))
    pm = jnp.exp(sb - ref)
    alpha = jnp.where(is_sm, jnp.exp(m_old - m_new), jnp.where(is_st, 1.0, 0.0))
    l_blk = jnp.sum(pm, axis=1, keepdims=True)
    l_s[...] = jnp.broadcast_to(alpha * l_s[:, :1] + l_blk, l_s.shape)
    m_s[...] = jnp.broadcast_to(jnp.where(is_sm, m_new, m_old), m_s.shape)

    pb = jnp.concatenate([pm, jnp.zeros((LANES - r_, 2 * PAGE_SIZE), F32)], axis=0).astype(BF16)
    pv = (_dot_nt(va_ref[0, 0].astype(BF16), pb[:, :PAGE_SIZE])
          + _dot_nt(vb_ref[0, 0].astype(BF16), pb[:, PAGE_SIZE:]))
    acc_s[...] = _col_to_row(alpha, LANES) * acc_s[...] + pv

    slab = pv[MOBA_COL:MOBA_COL + mw]
    head_of_row = lax.broadcasted_iota(jnp.int32, (mw, LANES), 0) // HEAD_DIM
    lane_m = lax.broadcasted_iota(jnp.int32, (mw, LANES), 1)
    part = jnp.sum(jnp.where(lane_m == 8 + head_of_row, slab, 0.0), axis=1, keepdims=True)
    mba_s[...] = jnp.where(lane_m == n, part, mba_s[...])
    mbm_s[...] = jnp.where(lane8 == n, m_blk[8:16], mbm_s[...])
    mbl_s[...] = jnp.where(lane8 == n, l_blk[8:16], mbl_s[...])
    ksum = jnp.sum(kta[MOBA_COL:MOBA_COL + mw] + ktb[MOBA_COL:MOBA_COL + mw], axis=1, keepdims=True)
    gq = ksum * qc_ref[0][MOBA_COL:MOBA_COL + mw] * (1.0 / MOBA_BLOCK)
    gtile = jnp.zeros((8, 1), F32)
    sub81 = lax.broadcasted_iota(jnp.int32, (8, 1), 0)
    for hh in range(H_MOBA):
        gh = jnp.sum(gq[hh * HEAD_DIM:(hh + 1) * HEAD_DIM], axis=0, keepdims=True)
        gtile = jnp.where(sub81 == hh, gh, gtile)
    gate_s[...] = jnp.where(lane8 == n, gtile, gate_s[...])

    @pl.when(p == nb - 1)
    def _():
        lv = lam_ref[...]
        lam = (jnp.exp(jnp.sum(lv[0:1] * lv[1:2], axis=1, keepdims=True))
               - jnp.exp(jnp.sum(lv[2:3] * lv[3:4], axis=1, keepdims=True)) + lam_init)
        l_row = jnp.where(lane1 < r_, _col_to_row(l_s[:, :1], LANES), 1.0)
        inv_l = 1.0 / l_row
        coef = jnp.where(lane1 < 8, jnp.where(lane1 % 2 == 0, inv_l, -lam * inv_l),
                         jnp.where(st_lane, 1.0, jnp.where(sm_lane, inv_l, 0.0)))
        rho = lax.broadcasted_iota(jnp.int32, (QKV_WIDTH, LANES), 0)
        lane_a = lax.broadcasted_iota(jnp.int32, (QKV_WIDTH, LANES), 1)
        pick = (((lane_a < 8) & (rho // (2 * HEAD_DIM) == lane_a // 2) & (rho < DIFF_WIDTH))
                | ((lane_a >= 12) & (lane_a < 20) & (rho // HEAD_DIM == _dec_row_slot(lane_a))))
        dense = jnp.sum(jnp.where(pick, acc_s[...] * coef, 0.0), axis=1, keepdims=True)
        pieces = []
        for hh in range(H_DIFF):
            x = dense[hh * 2 * HEAD_DIM:(hh + 1) * 2 * HEAD_DIM]
            y = x * lax.rsqrt(jnp.mean(x * x, axis=0, keepdims=True) + EPS) * subc_ref[...]
            pieces.append(y * (1.0 - lam_init))

        s_self = jnp.sum(qrow_s[...] * kn, axis=1, keepdims=True)
        sub_g = lax.broadcasted_iota(jnp.int32, (LANES, LANES), 0)
        lane_g = lax.broadcasted_iota(jnp.int32, (LANES, LANES), 1)
        for hh in range(H_MOBA):
            g_row = jnp.where(lane1 < nb, gate_s[hh:hh + 1, :], -jnp.inf)
            g_col = _row_to_col(g_row)
            beats = (g_col > g_row) | ((g_col == g_row) & (sub_g < lane_g))
            rank = jnp.sum(jnp.where(beats, 1.0, 0.0), axis=0, keepdims=True)
            sel = (rank < k_sel) & (lane1 < nb)
            m_h = mbm_s[hh:hh + 1, :]
            ss = s_self[8 + hh:9 + hh]
            top = jnp.maximum(jnp.max(jnp.where(sel, m_h, NEG), axis=1, keepdims=True), ss)
            w = jnp.where(sel, jnp.exp(m_h - top), 0.0)
            w_self = jnp.exp(ss - top)
            r0 = hh * HEAD_DIM
            num = (jnp.sum(mba_s[r0:r0 + HEAD_DIM, :] * w, axis=1, keepdims=True)
                   + w_self * vnc[MOBA_COL + r0:MOBA_COL + r0 + HEAD_DIM])
            den = jnp.sum(w * mbl_s[hh:hh + 1, :], axis=1, keepdims=True) + w_self
            pieces.append(num / den)
        o_ref[0] = jnp.concatenate(pieces + [dense[FOX_COL:]], axis=0)


def _decode_mixing(q_new, k_new, v_new, logf_new, cache_kt, cache_vt, cache_ft, page_table, lam,
                   subln, u_excl, layer_idx):
    bsz = q_new.shape[0]
    n_pages = page_table.shape[1]
    assert n_pages % 2 == 0 and 2 * PAGE_SIZE == MOBA_BLOCK
    nb = n_pages // 2
    assert nb <= LANES
    lam_init = 0.8 - 0.6 * math.exp(-0.3 * layer_idx)
    mw = H_MOBA * HEAD_DIM
    l_ = layer_idx
    page_a = lambda b, p, pt: (l_, pt[b, 2 * (nb - 1 - p)], 0, 0)
    page_b = lambda b, p, pt: (l_, pt[b, 2 * (nb - 1 - p) + 1], 0, 0)
    const = lambda b, p, pt: (0, 0)
    tok = lambda b, p, pt: (b, 0, 0)
    grid_spec = pltpu.PrefetchScalarGridSpec(
        num_scalar_prefetch=1,
        grid=(bsz, nb),
        in_specs=[
            pl.BlockSpec((4, HEAD_DIM), const),
            pl.BlockSpec((2 * HEAD_DIM, 1), const),
            pl.BlockSpec((1, 1, QKV_WIDTH), tok),
            pl.BlockSpec((1, QKV_WIDTH, 1), tok),
            pl.BlockSpec((1, 1, QKV_WIDTH), tok),
            pl.BlockSpec((1, QKV_WIDTH, 1), tok),
            pl.BlockSpec((1, 1, LANES), tok),
            pl.BlockSpec((1, 1, QKV_WIDTH, PAGE_SIZE), page_a),
            pl.BlockSpec((1, 1, QKV_WIDTH, PAGE_SIZE), page_b),
            pl.BlockSpec((1, 1, QKV_WIDTH, PAGE_SIZE), page_a),
            pl.BlockSpec((1, 1, QKV_WIDTH, PAGE_SIZE), page_b),
            pl.BlockSpec((1, 1, 8, PAGE_SIZE), page_a),
            pl.BlockSpec((1, 1, 8, PAGE_SIZE), page_b),
            pl.BlockSpec((MOBA_BLOCK, MOBA_BLOCK), const),
        ],
        out_specs=pl.BlockSpec((1, QKV_WIDTH, 1), tok),
        scratch_shapes=[
            pltpu.VMEM((DEC_ROWS, QKV_WIDTH), F32),
            pltpu.VMEM((DEC_ROWS, LANES), F32),
            pltpu.VMEM((DEC_ROWS, LANES), F32),
            pltpu.VMEM((QKV_WIDTH, LANES), F32),
            pltpu.VMEM((8, LANES), F32),
            pltpu.VMEM((8, LANES), F32),
            pltpu.VMEM((8, LANES), F32),
            pltpu.VMEM((8, LANES), F32),
            pltpu.VMEM((8, LANES), F32),
            pltpu.VMEM((mw, LANES), F32),
        ],
    )
    out = pl.pallas_call(
        functools.partial(_decode_kernel, nb=nb, k_sel=min(MOBA_TOPK, nb), lam_init=lam_init),
        grid_spec=grid_spec,
        out_shape=jax.ShapeDtypeStruct((bsz, QKV_WIDTH, 1), F32),
        compiler_params=_cparams(("parallel", "arbitrary")),
        name="decode_mix",
    )(page_table, lam, subln.reshape(2 * HEAD_DIM, 1), q_new.reshape(bsz, 1, QKV_WIDTH),
      q_new.reshape(bsz, QKV_WIDTH, 1), k_new.reshape(bsz, 1, QKV_WIDTH),
      v_new.reshape(bsz, QKV_WIDTH, 1), logf_new.reshape(bsz, 1, LANES), cache_kt, cache_kt,
      cache_vt, cache_vt, cache_ft, cache_ft, u_excl)
    return out.reshape(bsz, QKV_WIDTH)


def _rope_tables(pos):
    half = HEAD_DIM // 2
    inv = ROPE_THETA ** (-jnp.arange(half, dtype=F32) / half)
    ang = pos.astype(F32)[:, None] * inv[None, :]
    cos, sin = jnp.cos(ang), jnp.sin(ang)
    zero = jnp.zeros_like(sin)
    reps = ROPE_WIDTH // HEAD_DIM
    cos_t = jnp.tile(jnp.concatenate([cos, cos], axis=1), (1, reps))
    sin_a = jnp.tile(jnp.concatenate([-sin, zero], axis=1), (1, reps))
    sin_b = jnp.tile(jnp.concatenate([zero, sin], axis=1), (1, reps))
    return cos_t, sin_a, sin_b


def _tri(n, strict_lower):
    j = jnp.arange(n)[:, None]
    s = jnp.arange(n)[None, :]
    return ((j > s) if strict_lower else (j <= s)).astype(BF16)


def kernel(x_prompt, x_sample, cache_k, cache_v, cache_logf, cache_mem_k, cache_mem_v, page_table,
           mem_prompt, ffn1_norm, ffn1_w_up, ffn1_w_down, mix_norm, w_in, b_forget, diff_lambda,
           diff_subln, w_branch, w_gate, b_gate, w_out, xattn_norm, w_xq, w_xk, w_xv, w_xo,
           ffn2_norm, ffn2_w_up, ffn2_w_down, final_norm):
    bp, sp, d = x_prompt.shape
    bs, ss, _ = x_sample.shape
    assert ss == 1
    depth = ffn1_norm.shape[0]
    n_mem = mem_prompt.shape[1]
    n_pool = cache_k.shape[1]
    past_len = page_table.shape[1] * PAGE_SIZE

    pos_p = jnp.arange(sp, dtype=jnp.int32)
    pos_s = jnp.full((bs,), past_len, dtype=jnp.int32)
    rope_p = _rope_tables(pos_p)
    rope_s = _rope_tables(pos_s)
    u_excl = _tri(MOBA_BLOCK, True)
    u_incl = _tri(MOBA_BLOCK, False)

    cache_kt = jnp.transpose(cache_k, (0, 1, 3, 4, 2)).reshape(depth, n_pool, QKV_WIDTH, PAGE_SIZE)
    cache_vt = jnp.transpose(cache_v, (0, 1, 3, 4, 2)).reshape(depth, n_pool, QKV_WIDTH, PAGE_SIZE)
    cache_ft = jnp.pad(jnp.swapaxes(cache_logf, 2, 3), ((0, 0), (0, 0), (0, 8 - H_FOX), (0, 0)))

    h_p = x_prompt.reshape(bp * sp, d)
    h_s = x_sample.reshape(bs, d)
    mem2 = mem_prompt.reshape(bp * n_mem, d)
    outs = {k: [] for k in ("k_p", "v_p", "f_p", "mk_p", "mv_p", "k_s", "v_s", "f_s")}

    for l in range(depth):
        wu1 = ffn1_w_up[l].astype(BF16)
        wd1 = ffn1_w_down[l].astype(BF16)
        wu2 = ffn2_w_up[l].astype(BF16)
        wd2 = ffn2_w_down[l].astype(BF16)
        w_qkv = w_in[l][:, :3 * QKV_WIDTH].astype(BF16)
        w_f = jnp.pad(w_in[l][:, 3 * QKV_WIDTH:], ((0, 0), (0, LANES - H_FOX))).astype(BF16)
        b_f = jnp.pad(b_forget[l], (0, LANES - H_FOX)).reshape(1, LANES)
        wbr = w_branch[l].astype(BF16)
        wg = w_gate[l].astype(BF16)
        wo = w_out[l].astype(BF16)
        wxq = w_xq[l].astype(BF16)
        wxo = w_xo[l].astype(BF16)
        wxkv = jnp.concatenate([w_xk[l], w_xv[l]], axis=1).astype(BF16)
        last = l == depth - 1

        mem_kv = _matmul(mem2, wxkv)
        mem_k = mem_kv[:, :X_WIDTH].reshape(bp, n_mem, X_WIDTH)
        mem_v = mem_kv[:, X_WIDTH:].reshape(bp, n_mem, X_WIDTH)
        outs["mk_p"].append(mem_k.reshape(bp, n_mem, X_HEADS, HEAD_DIM))
        outs["mv_p"].append(mem_v.reshape(bp, n_mem, X_HEADS, HEAD_DIM))

        h_p = _ffn(h_p, ffn1_norm[l], wu1, wd1)
        q, k, v, lf = _in_proj(h_p, mix_norm[l], w_qkv, w_f, b_f, *rope_p)
        logf = lf[:, :H_FOX].reshape(bp, sp, H_FOX)
        outs["k_p"].append(k.reshape(bp, sp, N_SLOTS, HEAD_DIM))
        outs["v_p"].append(v.reshape(bp, sp, N_SLOTS, HEAD_DIM))
        outs["f_p"].append(logf)
        q3, k3, v3 = (a.reshape(bp, sp, QKV_WIDTH) for a in (q, k, v))
        f_cum = _cumsum_lanes(jnp.swapaxes(logf, 1, 2).reshape(bp * H_FOX, sp), u_incl)
        f_row = f_cum.reshape(bp, H_FOX // 2, 2, sp)
        f_col = jnp.swapaxes(f_row, 2, 3)
        o_a = _diff_attention(q3, k3, v3, diff_lambda[l], diff_subln[l], l)
        o_b = _moba_attention(q3, k3, v3)
        o_c = _fox_attention(q3, k3, v3, f_col, f_row)
        o_d = _stick_attention(q3, k3, v3, u_excl)
        flat = lambda o: o.reshape(bp * sp, o.shape[-1])
        h_p = _mix_out(h_p, mix_norm[l], [flat(o_a), flat(o_b), flat(o_c), flat(o_d)],
                       wbr, wg, b_gate[l], wo)
        h_p = _cross_attention(h_p.reshape(bp, sp, d), xattn_norm[l], wxq, mem_k, mem_v,
                               wxo).reshape(bp * sp, d)
        h_p = _ffn(h_p, ffn2_norm[l], wu2, wd2, final_norm if last else None)

        h_s = _ffn(h_s, ffn1_norm[l], wu1, wd1)
        q_s, k_s, v_s, lf_s = _in_proj(h_s, mix_norm[l], w_qkv, w_f, b_f, *rope_s)
        outs["k_s"].append(k_s.reshape(bs, 1, N_SLOTS, HEAD_DIM))
        outs["v_s"].append(v_s.reshape(bs, 1, N_SLOTS, HEAD_DIM))
        outs["f_s"].append(lf_s[:, :H_FOX].reshape(bs, 1, H_FOX))
        o_s = _decode_mixing(q_s, k_s, v_s, lf_s, cache_kt, cache_vt, cache_ft, page_table,
                             diff_lambda[l], diff_subln[l], u_excl, l)
        h_s = _mix_out(h_s, mix_norm[l],
                       [o_s[:, :MOBA_COL], o_s[:, MOBA_COL:FOX_COL], o_s[:, FOX_COL:STICK_COL],
                        o_s[:, STICK_COL:]], wbr, wg, b_gate[l], wo)
        h_s = _cross_attention(h_s.reshape(bs, 1, d), xattn_norm[l], wxq,
                               cache_mem_k[l].reshape(bs, n_mem, X_WIDTH),
                               cache_mem_v[l].reshape(bs, n_mem, X_WIDTH), wxo).reshape(bs, d)
        h_s = _ffn(h_s, ffn2_norm[l], wu2, wd2, final_norm if last else None)

    st = jnp.stack
    return (h_p.reshape(bp, sp, d), h_s.reshape(bs, 1, d), st(outs["k_p"]), st(outs["v_p"]),
            st(outs["f_p"]), st(outs["mk_p"]), st(outs["mv_p"]), st(outs["k_s"]), st(outs["v_s"]),
            st(outs["f_s"]))
```

```python
import functools
import math

import jax
import jax.numpy as jnp
from jax import lax
from jax.experimental import pallas as pl
from jax.experimental.pallas import tpu as pltpu

F32 = jnp.float32
BF16 = jnp.bfloat16

HEAD_DIM = 64
H_DIFF = 4
H_MOBA = 4
H_FOX = 4
H_STICK = 4
N_SLOTS = 2 * H_DIFF + H_MOBA + H_FOX + H_STICK
QKV_WIDTH = N_SLOTS * HEAD_DIM
ROPE_WIDTH = (2 * H_DIFF + H_MOBA) * HEAD_DIM
DIFF_WIDTH = 2 * H_DIFF * HEAD_DIM
MOBA_COL = DIFF_WIDTH
FOX_COL = MOBA_COL + H_MOBA * HEAD_DIM
STICK_COL = FOX_COL + H_FOX * HEAD_DIM
X_HEADS = 4
X_WIDTH = X_HEADS * HEAD_DIM
MOBA_BLOCK = 256
MOBA_TOPK = 3
ROPE_THETA = 10000.0
EPS = 1e-6
PAGE_SIZE = 128
LANES = 128
SCALE = HEAD_DIM ** -0.5
NEG = -1e30
VMEM_LIMIT = 56 * 1024 * 1024


def _cparams(sem):
    return pltpu.CompilerParams(dimension_semantics=sem, vmem_limit_bytes=VMEM_LIMIT)


def _dot(a, b):
    return jnp.dot(a, b, preferred_element_type=F32)


def _dot_nt(a, b):
    return lax.dot_general(a, b, (((1,), (1,)), ((), ())), preferred_element_type=F32)


def _split2(x):
    hi = x.astype(BF16)
    lo = (x - hi.astype(F32)).astype(BF16)
    return hi, lo


def _split3(x):
    hi = x.astype(BF16)
    r = x - hi.astype(F32)
    mid = r.astype(BF16)
    lo = (r - mid.astype(F32)).astype(BF16)
    return hi, mid, lo


def _dot_01(x, u01):
    a, b, c = _split3(x)
    return _dot(a, u01) + _dot(b, u01) + _dot(c, u01)


def _dot_nt_hi(a, b):
    a1, a2 = _split2(a)
    b1, b2 = _split2(b)
    return _dot_nt(a1, b1) + _dot_nt(a1, b2) + _dot_nt(a2, b1)


def _rms(x, g):
    return x * lax.rsqrt(jnp.mean(x * x, axis=-1, keepdims=True) + EPS) * g


def _sigmoid(x):
    return 1.0 / (1.0 + jnp.exp(-x))


def _neg_softplus(z):
    return -(jnp.maximum(z, 0.0) + jnp.log(1.0 + jnp.exp(-jnp.abs(z))))


def _log_sigmoid(z):
    return jnp.minimum(z, 0.0) - jnp.log(1.0 + jnp.exp(-jnp.abs(z)))


def _row_tile(t, want):
    tm = min(t, want)
    assert t % tm == 0
    return tm


def _ffn_kernel(*refs, n_chunks, final):
    if final:
        h_ref, g_ref, wa_ref, wb_ref, wd_ref, fg_ref, o_ref, n_s, acc_s = refs
    else:
        h_ref, g_ref, wa_ref, wb_ref, wd_ref, o_ref, n_s, acc_s = refs
    c = pl.program_id(1)

    @pl.when(c == 0)
    def _():
        n_s[...] = _rms(h_ref[...], g_ref[...]).astype(BF16)
        acc_s[...] = jnp.zeros_like(acc_s)

    n = n_s[...]
    a = _dot(n, wa_ref[...])
    b = _dot(n, wb_ref[...])
    act = (a * _sigmoid(a) * b).astype(BF16)
    acc_s[...] += _dot(act, wd_ref[...])

    @pl.when(c == n_chunks - 1)
    def _():
        y = h_ref[...] + 0.5 * acc_s[...]
        if final:
            y = _rms(y, fg_ref[...])
        o_ref[...] = y


def _ffn(h, g, w_up, w_down, final_g=None):
    t, d = h.shape
    d_ff = w_down.shape[0]
    tm = _row_tile(t, 512)
    tf = d_ff // 2 if (d_ff // 2) % LANES == 0 else d_ff
    n_chunks = d_ff // tf
    final = final_g is not None
    in_specs = [
        pl.BlockSpec((tm, d), lambda i, c: (i, 0)),
        pl.BlockSpec((1, d), lambda i, c: (0, 0)),
        pl.BlockSpec((d, tf), lambda i, c: (0, c)),
        pl.BlockSpec((d, tf), lambda i, c: (0, n_chunks + c)),
        pl.BlockSpec((tf, d), lambda i, c: (c, 0)),
    ]
    args = [h, g.reshape(1, d), w_up, w_up, w_down]
    if final:
        in_specs.append(pl.BlockSpec((1, d), lambda i, c: (0, 0)))
        args.append(final_g.reshape(1, d))
    return pl.pallas_call(
        functools.partial(_ffn_kernel, n_chunks=n_chunks, final=final),
        grid=(t // tm, n_chunks),
        in_specs=in_specs,
        out_specs=pl.BlockSpec((tm, d), lambda i, c: (i, 0)),
        out_shape=jax.ShapeDtypeStruct((t, d), F32),
        scratch_shapes=[pltpu.VMEM((tm, d), BF16), pltpu.VMEM((tm, d), F32)],
        compiler_params=_cparams(("parallel", "arbitrary")),
        name="ffn",
    )(*args)


def _inproj_kernel(h_ref, g_ref, w_ref, wf_ref, bf_ref, cos_ref, sa_ref, sb_ref,
                   q_ref, k_ref, v_ref, lf_ref, n_s):
    j = pl.program_id(1)

    @pl.when(j == 0)
    def _():
        n = _rms(h_ref[...], g_ref[...]).astype(BF16)
        n_s[...] = n
        lf_ref[...] = _log_sigmoid(_dot(n, wf_ref[...]) + bf_ref[...])

    y = _dot(n_s[...], w_ref[...])

    def rotary(o_ref):
        x = y[:, :ROPE_WIDTH]
        r = (x * cos_ref[...]
             + pltpu.roll(x, ROPE_WIDTH - HEAD_DIM // 2, 1) * sa_ref[...]
             + pltpu.roll(x, HEAD_DIM // 2, 1) * sb_ref[...])
        o_ref[:, :ROPE_WIDTH] = r
        o_ref[:, ROPE_WIDTH:] = y[:, ROPE_WIDTH:]

    @pl.when(j == 0)
    def _():
        rotary(q_ref)

    @pl.when(j == 1)
    def _():
        rotary(k_ref)

    @pl.when(j == 2)
    def _():
        v_ref[...] = y


def _in_proj(h, g, w_qkv, w_f, b_f, cos_t, sin_a, sin_b):
    t, d = h.shape
    tm = _row_tile(t, 512)
    n_pos = cos_t.shape[0] // tm
    assert cos_t.shape[0] % tm == 0
    pos_map = lambda i, j: (i % n_pos, 0)
    return pl.pallas_call(
        _inproj_kernel,
        grid=(t // tm, 3),
        in_specs=[
            pl.BlockSpec((tm, d), lambda i, j: (i, 0)),
            pl.BlockSpec((1, d), lambda i, j: (0, 0)),
            pl.BlockSpec((d, QKV_WIDTH), lambda i, j: (0, j)),
            pl.BlockSpec((d, LANES), lambda i, j: (0, 0)),
            pl.BlockSpec((1, LANES), lambda i, j: (0, 0)),
            pl.BlockSpec((tm, ROPE_WIDTH), pos_map),
            pl.BlockSpec((tm, ROPE_WIDTH), pos_map),
            pl.BlockSpec((tm, ROPE_WIDTH), pos_map),
        ],
        out_specs=[pl.BlockSpec((tm, QKV_WIDTH), lambda i, j: (i, 0))] * 3
        + [pl.BlockSpec((tm, LANES), lambda i, j: (i, 0))],
        out_shape=[jax.ShapeDtypeStruct((t, QKV_WIDTH), F32)] * 3
        + [jax.ShapeDtypeStruct((t, LANES), F32)],
        scratch_shapes=[pltpu.VMEM((tm, d), BF16)],
        compiler_params=_cparams(("parallel", "arbitrary")),
        name="in_proj",
    )(h, g.reshape(1, d), w_qkv, w_f, b_f, cos_t, sin_a, sin_b)


def _matmul_kernel(x_ref, w_ref, o_ref):
    o_ref[...] = _dot(x_ref[...].astype(BF16), w_ref[...])


def _matmul(x, w):
    t, d = x.shape
    n = w.shape[1]
    tm = _row_tile(t, 512)
    return pl.pallas_call(
        _matmul_kernel,
        grid=(t // tm,),
        in_specs=[pl.BlockSpec((tm, d), lambda i: (i, 0)),
                  pl.BlockSpec((d, n), lambda i: (0, 0))],
        out_specs=pl.BlockSpec((tm, n), lambda i: (i, 0)),
        out_shape=jax.ShapeDtypeStruct((t, n), F32),
        compiler_params=_cparams(("parallel",)),
        name="mem_kv_proj",
    )(x, w)


def _lane_masks(shape):
    lane = lax.broadcasted_iota(jnp.int32, shape, 1)
    return lane < HEAD_DIM, lane >= HEAD_DIM


def _tri_mask(tq, strict):
    row = lax.broadcasted_iota(jnp.int32, (tq, tq), 0)
    col = lax.broadcasted_iota(jnp.int32, (tq, tq), 1)
    return (col < row) if strict else (col <= row)


def _mask_last_block(s, tq, keep):
    last = jnp.where(keep, s[:, -tq:], NEG)
    if s.shape[1] == tq:
        return last
    return jnp.concatenate([s[:, :-tq], last], axis=1)


def _softmax_pv(s, vb):
    p = jnp.exp(s - jnp.max(s, axis=1, keepdims=True))
    return _dot(p.astype(BF16), vb) / jnp.sum(p, axis=1, keepdims=True)


def _for_each_query_block(nq, fn):
    i = pl.program_id(2)
    for n in range(nq):
        pl.when(i == n)(functools.partial(fn, n))


def _diff_kernel(lam_ref, sub_ref, q_ref, k_ref, v_ref, o_ref, *, tq, nq, lam_init):
    def branch(n):
        length = (n + 1) * tq
        q = q_ref[0] * SCALE
        lo, hi = _lane_masks(q.shape)
        kb = k_ref[0, :length, :].astype(BF16)
        vb = v_ref[0, :length, :].astype(BF16)
        keep = _tri_mask(tq, False)
        o1 = _softmax_pv(_mask_last_block(_dot_nt(jnp.where(lo, q, 0.0).astype(BF16), kb), tq, keep), vb)
        o2 = _softmax_pv(_mask_last_block(_dot_nt(jnp.where(hi, q, 0.0).astype(BF16), kb), tq, keep), vb)
        lv = lam_ref[...]
        lam = (jnp.exp(jnp.sum(lv[0:1] * lv[1:2], axis=1, keepdims=True))
               - jnp.exp(jnp.sum(lv[2:3] * lv[3:4], axis=1, keepdims=True)) + lam_init)
        o = _rms(o1 - lam * o2, sub_ref[...]) * (1.0 - lam_init)
        o_ref[0] = o.astype(o_ref.dtype)

    _for_each_query_block(nq, branch)


def _diff_attention(q3, k3, v3, lam, subln, layer_idx):
    bsz, s, _ = q3.shape
    tq = MOBA_BLOCK
    assert s % tq == 0
    lam_init = 0.8 - 0.6 * math.exp(-0.3 * layer_idx)
    return pl.pallas_call(
        functools.partial(_diff_kernel, tq=tq, nq=s // tq, lam_init=lam_init),
        grid=(bsz, H_DIFF, s // tq),
        in_specs=[
            pl.BlockSpec((4, HEAD_DIM), lambda b, g, i: (0, 0)),
            pl.BlockSpec((1, 2 * HEAD_DIM), lambda b, g, i: (0, 0)),
            pl.BlockSpec((1, tq, LANES), lambda b, g, i: (b, i, g)),
            pl.BlockSpec((1, s, LANES), lambda b, g, i: (b, 0, g)),
            pl.BlockSpec((1, s, LANES), lambda b, g, i: (b, 0, g)),
        ],
        out_specs=pl.BlockSpec((1, tq, LANES), lambda b, g, i: (b, i, g)),
        out_shape=jax.ShapeDtypeStruct((bsz, s, DIFF_WIDTH), BF16),
        compiler_params=_cparams(("parallel", "parallel", "arbitrary")),
        name="diff_attn",
    )(lam, subln.reshape(1, 2 * HEAD_DIM), q3, k3, v3)


def _moba_kernel(q_ref, k_ref, v_ref, o_ref, kmean_s, *, tq, nq, k_sel):
    i = pl.program_id(2)

    @pl.when(i == 0)
    def _():
        kmean_s[...] = jnp.zeros_like(kmean_s)
        for n in range(nq):
            blk = k_ref[0, n * MOBA_BLOCK:(n + 1) * MOBA_BLOCK, :]
            kmean_s[n:n + 1, :] = jnp.sum(blk, axis=0, keepdims=True) * (1.0 / MOBA_BLOCK)

    def branch(n):
        length = (n + 1) * tq
        q = q_ref[0] * SCALE
        lane = lax.broadcasted_iota(jnp.int32, (tq, LANES), 1)
        kb = k_ref[0, :length, :].astype(BF16)
        vb = v_ref[0, :length, :].astype(BF16)
        keep = _tri_mask(tq, False)
        kmean = kmean_s[...]
        outs = []
        for hmask in _lane_masks(q.shape):
            qh = jnp.where(hmask, q, 0.0)
            s = _dot_nt(qh.astype(BF16), kb)
            pieces = []
            if n > 0:
                gate = jnp.where(lane < n, _dot_nt_hi(qh, kmean), -jnp.inf)
                rank = jnp.zeros((tq, LANES), F32)
                for m in range(n):
                    gm = gate[:, m:m + 1]
                    beats = (gm > gate) | ((gm == gate) & (lane > m))
                    rank = rank + jnp.where(beats, 1.0, 0.0)
                for m in range(n):
                    hidden = jnp.where(rank[:, m:m + 1] < k_sel, 0.0, NEG)
                    pieces.append(s[:, m * tq:(m + 1) * tq] + hidden)
            pieces.append(jnp.where(keep, s[:, n * tq:], NEG))
            s = pieces[0] if n == 0 else jnp.concatenate(pieces, axis=1)
            outs.append(_softmax_pv(s, vb))
        o_ref[0] = jnp.where(lane < HEAD_DIM, outs[0], outs[1]).astype(o_ref.dtype)

    _for_each_query_block(nq, branch)


def _moba_attention(q3, k3, v3):
    bsz, s, _ = q3.shape
    tq = MOBA_BLOCK
    n_full = s // MOBA_BLOCK
    assert s % tq == 0 and n_full <= LANES
    c0 = MOBA_COL // LANES
    return pl.pallas_call(
        functools.partial(_moba_kernel, tq=tq, nq=n_full, k_sel=min(MOBA_TOPK, n_full)),
        grid=(bsz, H_MOBA // 2, s // tq),
        in_specs=[
            pl.BlockSpec((1, tq, LANES), lambda b, g, i: (b, i, c0 + g)),
            pl.BlockSpec((1, s, LANES), lambda b, g, i: (b, 0, c0 + g)),
            pl.BlockSpec((1, s, LANES), lambda b, g, i: (b, 0, c0 + g)),
        ],
        out_specs=pl.BlockSpec((1, tq, LANES), lambda b, g, i: (b, i, g)),
        out_shape=jax.ShapeDtypeStruct((bsz, s, H_MOBA * HEAD_DIM), BF16),
        scratch_shapes=[pltpu.VMEM((LANES, LANES), F32)],
        compiler_params=_cparams(("parallel", "parallel", "arbitrary")),
        name="moba_attn",
    )(q3, k3, v3)


def _fox_kernel(q_ref, k_ref, v_ref, fq_ref, fk_ref, o_ref, *, tq, nq):
    def branch(n):
        length = (n + 1) * tq
        q = q_ref[0] * SCALE
        lo, hi = _lane_masks(q.shape)
        kb = k_ref[0, :length, :].astype(BF16)
        vb = v_ref[0, :length, :].astype(BF16)
        keep = _tri_mask(tq, False)
        fq = fq_ref[0, 0]
        outs = []
        for hh, hmask in enumerate((lo, hi)):
            fk = fk_ref[0, 0, hh:hh + 1, :length]
            s = _dot_nt(jnp.where(hmask, q, 0.0).astype(BF16), kb) + (fq[:, hh:hh + 1] - fk)
            outs.append(_softmax_pv(_mask_last_block(s, tq, keep), vb))
        o_ref[0] = jnp.where(lo, outs[0], outs[1]).astype(o_ref.dtype)

    _for_each_query_block(nq, branch)


def _fox_attention(q3, k3, v3, f_col, f_row):
    bsz, s, _ = q3.shape
    tq = MOBA_BLOCK
    c0 = FOX_COL // LANES
    return pl.pallas_call(
        functools.partial(_fox_kernel, tq=tq, nq=s // tq),
        grid=(bsz, H_FOX // 2, s // tq),
        in_specs=[
            pl.BlockSpec((1, tq, LANES), lambda b, g, i: (b, i, c0 + g)),
            pl.BlockSpec((1, s, LANES), lambda b, g, i: (b, 0, c0 + g)),
            pl.BlockSpec((1, s, LANES), lambda b, g, i: (b, 0, c0 + g)),
            pl.BlockSpec((1, 1, tq, 2), lambda b, g, i: (b, g, i, 0)),
            pl.BlockSpec((1, 1, 2, s), lambda b, g, i: (b, g, 0, 0)),
        ],
        out_specs=pl.BlockSpec((1, tq, LANES), lambda b, g, i: (b, i, g)),
        out_shape=jax.ShapeDtypeStruct((bsz, s, H_FOX * HEAD_DIM), BF16),
        compiler_params=_cparams(("parallel", "parallel", "arbitrary")),
        name="fox_attn",
    )(q3, k3, v3, f_col, f_row)


def _stick_kernel(q_ref, k_ref, v_ref, u_ref, o_ref, *, tq, nq):
    def branch(n):
        length = (n + 1) * tq
        q = q_ref[0] * SCALE
        lo, hi = _lane_masks(q.shape)
        kb = k_ref[0, :length, :].astype(BF16)
        vb = v_ref[0, :length, :].astype(BF16)
        strict = _tri_mask(tq, True)
        u01 = u_ref[...]
        outs = []
        for hmask in (lo, hi):
            z = _dot_nt(jnp.where(hmask, q, 0.0).astype(BF16), kb)
            carry = jnp.zeros((tq, 1), F32)
            weights = [None] * (n + 1)
            for m in range(n, -1, -1):
                zm = z[:, m * tq:(m + 1) * tq]
                lr = _neg_softplus(zm)
                if m == n:
                    lr = jnp.where(strict, lr, 0.0)
                e = zm + lr + carry + _dot_01(lr, u01)
                if m == n:
                    e = jnp.where(strict, e, NEG)
                weights[m] = jnp.exp(e).astype(BF16)
                carry = carry + jnp.sum(lr, axis=1, keepdims=True)
            w = weights[0] if n == 0 else jnp.concatenate(weights, axis=1)
            outs.append(_dot(w, vb))
        o_ref[0] = jnp.where(lo, outs[0], outs[1]).astype(o_ref.dtype)

    _for_each_query_block(nq, branch)


def _stick_attention(q3, k3, v3, u01):
    bsz, s, _ = q3.shape
    tq = MOBA_BLOCK
    c0 = STICK_COL // LANES
    return pl.pallas_call(
        functools.partial(_stick_kernel, tq=tq, nq=s // tq),
        grid=(bsz, H_STICK // 2, s // tq),
        in_specs=[
            pl.BlockSpec((1, tq, LANES), lambda b, g, i: (b, i, c0 + g)),
            pl.BlockSpec((1, s, LANES), lambda b, g, i: (b, 0, c0 + g)),
            pl.BlockSpec((1, s, LANES), lambda b, g, i: (b, 0, c0 + g)),
            pl.BlockSpec((tq, tq), lambda b, g, i: (0, 0)),
        ],
        out_specs=pl.BlockSpec((1, tq, LANES), lambda b, g, i: (b, i, g)),
        out_shape=jax.ShapeDtypeStruct((bsz, s, H_STICK * HEAD_DIM), BF16),
        compiler_params=_cparams(("parallel", "parallel", "arbitrary")),
        name="stick_attn",
    )(q3, k3, v3, u01)


def _cumsum_kernel(x_ref, u_ref, o_ref, *, chunk):
    rows, s = x_ref.shape
    carry = jnp.zeros((rows, 1), F32)
    for c in range(s // chunk):
        x = x_ref[:, c * chunk:(c + 1) * chunk]
        o_ref[:, c * chunk:(c + 1) * chunk] = carry + _dot_01(x, u_ref[...])
        carry = carry + jnp.sum(x, axis=1, keepdims=True)


def _cumsum_lanes(x, u_incl):
    rows, s = x.shape
    chunk = u_incl.shape[0]
    assert s % chunk == 0
    return pl.pallas_call(
        functools.partial(_cumsum_kernel, chunk=chunk),
        out_shape=jax.ShapeDtypeStruct((rows, s), F32),
        name="logf_cumsum",
    )(x, u_incl)


def _mix_kernel(h_ref, g_ref, oa_ref, ob_ref, oc_ref, od_ref, wa_ref, wb_ref, wc_ref, wd_ref,
                wg_ref, bg_ref, wo_ref, out_ref):
    h = h_ref[...]
    d = h.shape[1]
    n = _rms(h, g_ref[...]).astype(BF16)
    acc = jnp.zeros(h.shape, F32)
    branches = ((oa_ref, wa_ref), (ob_ref, wb_ref), (oc_ref, wc_ref), (od_ref, wd_ref))
    for b, (o_ref, w_ref) in enumerate(branches):
        gate = _sigmoid(_dot(n, wg_ref[:, b * d:(b + 1) * d]) + bg_ref[:, b * d:(b + 1) * d])
        acc = acc + gate * _dot(o_ref[...].astype(BF16), w_ref[...])
    out_ref[...] = h + _dot(acc.astype(BF16), wo_ref[...])


def _mix_out(h, g, outs, w_branch, w_gate, b_gate, w_out):
    t, d = h.shape
    tm = _row_tile(t, 512)
    widths = [o.shape[1] for o in outs]
    offs = [0]
    for w in widths:
        offs.append(offs[-1] + w)
    wbs = [w_branch[offs[b]:offs[b + 1]] for b in range(4)]
    row = lambda i: (i, 0)
    const = lambda i: (0, 0)
    return pl.pallas_call(
        _mix_kernel,
        grid=(t // tm,),
        in_specs=([pl.BlockSpec((tm, d), row), pl.BlockSpec((1, d), const)]
                  + [pl.BlockSpec((tm, w), row) for w in widths]
                  + [pl.BlockSpec((w, d), const) for w in widths]
                  + [pl.BlockSpec((d, 4 * d), const), pl.BlockSpec((1, 4 * d), const),
                     pl.BlockSpec((d, d), const)]),
        out_specs=pl.BlockSpec((tm, d), row),
        out_shape=jax.ShapeDtypeStruct((t, d), F32),
        compiler_params=_cparams(("parallel",)),
        name="mix_out",
    )(h, g.reshape(1, d), *outs, *wbs, w_gate, b_gate.reshape(1, 4 * d), w_out)


def _xattn_kernel(h_ref, g_ref, wq_ref, mk_ref, mv_ref, wo_ref, out_ref):
    rows = h_ref.shape[1]
    h = h_ref[0]
    if rows < 8:
        h = jnp.broadcast_to(h[:1], (8, h.shape[1]))
    n = _rms(h, g_ref[...]).astype(BF16)
    q = _dot(n, wq_ref[...]) * SCALE
    mk = mk_ref[0].astype(BF16)
    mv = mv_ref[0].astype(BF16)
    lane = lax.broadcasted_iota(jnp.int32, q.shape, 1)
    o = jnp.zeros(q.shape, F32)
    for hh in range(X_HEADS):
        own = (lane >= hh * HEAD_DIM) & (lane < (hh + 1) * HEAD_DIM)
        s = _dot_nt(jnp.where(own, q, 0.0).astype(BF16), mk)
        p = jnp.exp(s - jnp.max(s, axis=1, keepdims=True))
        pv = _dot(p.astype(BF16), mv) / jnp.sum(p, axis=1, keepdims=True)
        o = jnp.where(own, pv, o)
    out_ref[0] = (h + _dot(o.astype(BF16), wo_ref[...]))[:rows]


def _cross_attention(h3, g, w_xq, mem_k, mem_v, w_xo):
    bsz, s, d = h3.shape
    n_mem = mem_k.shape[1]
    assert s == 1 or s % 8 == 0
    tm = _row_tile(s, 512)
    return pl.pallas_call(
        _xattn_kernel,
        grid=(bsz, s // tm),
        in_specs=[
            pl.BlockSpec((1, tm, d), lambda b, i: (b, i, 0)),
            pl.BlockSpec((1, d), lambda b, i: (0, 0)),
            pl.BlockSpec((d, X_WIDTH), lambda b, i: (0, 0)),
            pl.BlockSpec((1, n_mem, X_WIDTH), lambda b, i: (b, 0, 0)),
            pl.BlockSpec((1, n_mem, X_WIDTH), lambda b, i: (b, 0, 0)),
            pl.BlockSpec((X_WIDTH, d), lambda b, i: (0, 0)),
        ],
        out_specs=pl.BlockSpec((1, tm, d), lambda b, i: (b, i, 0)),
        out_shape=jax.ShapeDtypeStruct((bsz, s, d), F32),
        compiler_params=_cparams(("parallel", "parallel")),
        name="cross_attn",
    )(h3, g.reshape(1, d), w_xq, mem_k, mem_v, w_xo)


DEC_ROWS = 32


def _dec_row_slot(r):
    return jnp.where(r < 12, r, jnp.where(r < 16, r + 4, jnp.where(r < 20, r - 4, -1)))


def _col_to_row(col, width):
    n = col.shape[0]
    sub = lax.broadcasted_iota(jnp.int32, (n, width), 0)
    lane = lax.broadcasted_iota(jnp.int32, (n, width), 1)
    return jnp.sum(jnp.where(sub == lane, jnp.broadcast_to(col, (n, width)), 0.0), axis=0, keepdims=True)


def _row_to_col(row):
    n = row.shape[1]
    sub = lax.broadcasted_iota(jnp.int32, (n, n), 0)
    lane = lax.broadcasted_iota(jnp.int32, (n, n), 1)
    return jnp.sum(jnp.where(sub == lane, jnp.broadcast_to(row, (n, n)), 0.0), axis=1, keepdims=True)


def _decode_kernel(pt_ref, lam_ref, subc_ref, q_ref, qc_ref, kn_ref, vnc_ref, fn_ref, *rest,
                   nb, per_step, k_sel, lam_init):
    del pt_ref
    page_refs = rest[:6 * per_step]
    u_ref, o_ref = rest[6 * per_step:6 * per_step + 2]
    qrow_s, m_s, l_s, acc_s, cs_s, cf_s, gate_s, mbm_s, mbl_s, mba_s = rest[6 * per_step + 2:]
    p = pl.program_id(1)
    n_steps = nb // per_step
    r_ = DEC_ROWS
    rowk = lax.broadcasted_iota(jnp.int32, (r_, 1), 0)
    is_sm = (rowk < 8) | ((rowk >= 16) & (rowk < 20))
    is_mb = (rowk >= 8) & (rowk < 12)
    is_st = (rowk >= 12) & (rowk < 16)
    lane1 = lax.broadcasted_iota(jnp.int32, (1, LANES), 1)
    sm_lane = (lane1 < 8) | ((lane1 >= 16) & (lane1 < 20))
    st_lane = (lane1 >= 12) & (lane1 < 16)
    lane8 = lax.broadcasted_iota(jnp.int32, (8, LANES), 1)
    sub8 = lax.broadcasted_iota(jnp.int32, (8, LANES), 0)
    kn = kn_ref[0]
    vnc = vnc_ref[0]
    mw = H_MOBA * HEAD_DIM

    @pl.when(p == 0)
    def _():
        rowi = lax.broadcasted_iota(jnp.int32, (r_, QKV_WIDTH), 0)
        coli = lax.broadcasted_iota(jnp.int32, (r_, QKV_WIDTH), 1)
        qrow = jnp.where((coli // HEAD_DIM) == _dec_row_slot(rowi), q_ref[0] * SCALE, 0.0)
        qrow_s[...] = qrow
        s_self = jnp.sum(qrow * kn, axis=1, keepdims=True)
        m_s[...] = jnp.broadcast_to(s_self, m_s.shape)
        l_s[...] = jnp.ones_like(l_s)
        acc_s[...] = jnp.where(sm_lane, jnp.broadcast_to(vnc, acc_s.shape), 0.0)
        cs_s[...] = jnp.zeros_like(cs_s)
        f_new = jnp.sum(jnp.where(sub8 == lane8, jnp.broadcast_to(fn_ref[0], (8, LANES)), 0.0),
                        axis=1, keepdims=True)
        cf_s[...] = jnp.broadcast_to(f_new, cf_s.shape)
        gate_s[...] = jnp.zeros_like(gate_s)
        mbm_s[...] = jnp.zeros_like(mbm_s)
        mbl_s[...] = jnp.zeros_like(mbl_s)
        mba_s[...] = jnp.zeros_like(mba_s)

    for blk in range(per_step):
        _decode_block(nb - 1 - (p * per_step + blk), page_refs[6 * blk:6 * blk + 6], u_ref, qc_ref,
                      qrow_s, m_s, l_s, acc_s, cs_s, cf_s, gate_s, mbm_s, mbl_s, mba_s)

    @pl.when(p == n_steps - 1)
    def _():
        _decode_finish(lam_ref, subc_ref, kn, vnc, qrow_s, l_s, acc_s, gate_s, mbm_s, mbl_s, mba_s,
                       o_ref, nb=nb, k_sel=k_sel, lam_init=lam_init)


def _decode_block(n, pages, u_ref, qc_ref, qrow_s, m_s, l_s, acc_s, cs_s, cf_s, gate_s, mbm_s, mbl_s,
                  mba_s):
    ka_ref, kb_ref, va_ref, vb_ref, fa_ref, fb_ref = pages
    r_ = DEC_ROWS
    mw = H_MOBA * HEAD_DIM
    rowk = lax.broadcasted_iota(jnp.int32, (r_, 1), 0)
    is_sm = (rowk < 8) | ((rowk >= 16) & (rowk < 20))
    is_mb = (rowk >= 8) & (rowk < 12)
    is_st = (rowk >= 12) & (rowk < 16)
    lane8 = lax.broadcasted_iota(jnp.int32, (8, LANES), 1)
    qb = qrow_s[...].astype(BF16)
    kta = ka_ref[0, 0]
    ktb = kb_ref[0, 0]
    s = jnp.concatenate([_dot(qb, kta.astype(BF16)), _dot(qb, ktb.astype(BF16))], axis=1)
    u01 = u_ref[...]

    lf = jnp.concatenate([fa_ref[0, 0], fb_ref[0, 0]], axis=1)
    cf = cf_s[:, :1]
    g_bias = cf + _dot_01(lf, u01)
    cf_s[...] = jnp.broadcast_to(cf + jnp.sum(lf, axis=1, keepdims=True), cf_s.shape)

    z8 = s[8:16]
    sub8w = lax.broadcasted_iota(jnp.int32, z8.shape, 0)
    lr = jnp.where(sub8w >= 4, _neg_softplus(z8), 0.0)
    cs = cs_s[:, :1]
    add8 = jnp.where(sub8w >= 4, lr + cs + _dot_01(lr, u01), 0.0)
    cs_s[...] = jnp.broadcast_to(cs + jnp.sum(lr, axis=1, keepdims=True), cs_s.shape)

    zeros8 = jnp.zeros_like(z8)
    sb = s + jnp.concatenate([zeros8, add8, g_bias, zeros8], axis=0)
    m_blk = jnp.max(sb, axis=1, keepdims=True)
    m_old = m_s[:, :1]
    m_new = jnp.maximum(m_old, m_blk)
    ref = jnp.where(is_sm, m_new, jnp.where(is_mb, m_blk, 0.0))
    pm = jnp.exp(sb - ref)
    alpha = jnp.where(is_sm, jnp.exp(m_old - m_new), jnp.where(is_st, 1.0, 0.0))
    l_blk = jnp.sum(pm, axis=1, keepdims=True)
    l_s[...] = jnp.broadcast_to(alpha * l_s[:, :1] + l_blk, l_s.shape)
    m_s[...] = jnp.broadcast_to(jnp.where(is_sm, m_new, m_old), m_s.shape)

    pb = jnp.concatenate([pm, jnp.zeros((LANES - r_, 2 * PAGE_SIZE), F32)], axis=0).astype(BF16)
    pv = (_dot_nt(va_ref[0, 0].astype(BF16), pb[:, :PAGE_SIZE])
          + _dot_nt(vb_ref[0, 0].astype(BF16), pb[:, PAGE_SIZE:]))
    acc_s[...] = _col_to_row(alpha, LANES) * acc_s[...] + pv

    slab = pv[MOBA_COL:MOBA_COL + mw]
    head_of_row = lax.broadcasted_iota(jnp.int32, (mw, LANES), 0) // HEAD_DIM
    lane_m = lax.broadcasted_iota(jnp.int32, (mw, LANES), 1)
    part = jnp.sum(jnp.where(lane_m == 8 + head_of_row, slab, 0.0), axis=1, keepdims=True)
    mba_s[...] = jnp.where(lane_m == n, part, mba_s[...])
    mbm_s[...] = jnp.where(lane8 == n, m_blk[8:16], mbm_s[...])
    mbl_s[...] = jnp.where(lane8 == n, l_blk[8:16], mbl_s[...])
    ksum = jnp.sum(kta[MOBA_COL:MOBA_COL + mw] + ktb[MOBA_COL:MOBA_COL + mw], axis=1, keepdims=True)
    gq = ksum * qc_ref[0][MOBA_COL:MOBA_COL + mw] * (1.0 / MOBA_BLOCK)
    gtile = jnp.zeros((8, 1), F32)
    sub81 = lax.broadcasted_iota(jnp.int32, (8, 1), 0)
    for hh in range(H_MOBA):
        gh = jnp.sum(gq[hh * HEAD_DIM:(hh + 1) * HEAD_DIM], axis=0, keepdims=True)
        gtile = jnp.where(sub81 == hh, gh, gtile)
    gate_s[...] = jnp.where(lane8 == n, gtile, gate_s[...])


def _decode_finish(lam_ref, subc_ref, kn, vnc, qrow_s, l_s, acc_s, gate_s, mbm_s, mbl_s, mba_s, o_ref,
                   *, nb, k_sel, lam_init):
    r_ = DEC_ROWS
    lane1 = lax.broadcasted_iota(jnp.int32, (1, LANES), 1)
    sm_lane = (lane1 < 8) | ((lane1 >= 16) & (lane1 < 20))
    st_lane = (lane1 >= 12) & (lane1 < 16)
    lv = lam_ref[...]
    lam = (jnp.exp(jnp.sum(lv[0:1] * lv[1:2], axis=1, keepdims=True))
           - jnp.exp(jnp.sum(lv[2:3] * lv[3:4], axis=1, keepdims=True)) + lam_init)
    l_row = jnp.where(lane1 < r_, _col_to_row(l_s[:, :1], LANES), 1.0)
    inv_l = 1.0 / l_row
    coef = jnp.where(lane1 < 8, jnp.where(lane1 % 2 == 0, inv_l, -lam * inv_l),
                     jnp.where(st_lane, 1.0, jnp.where(sm_lane, inv_l, 0.0)))
    rho = lax.broadcasted_iota(jnp.int32, (QKV_WIDTH, LANES), 0)
    lane_a = lax.broadcasted_iota(jnp.int32, (QKV_WIDTH, LANES), 1)
    pick = (((lane_a < 8) & (rho // (2 * HEAD_DIM) == lane_a // 2) & (rho < DIFF_WIDTH))
            | ((lane_a >= 12) & (lane_a < 20) & (rho // HEAD_DIM == _dec_row_slot(lane_a))))
    dense = jnp.sum(jnp.where(pick, acc_s[...] * coef, 0.0), axis=1, keepdims=True)
    pieces = []
    for hh in range(H_DIFF):
        x = dense[hh * 2 * HEAD_DIM:(hh + 1) * 2 * HEAD_DIM]
        y = x * lax.rsqrt(jnp.mean(x * x, axis=0, keepdims=True) + EPS) * subc_ref[...]
        pieces.append(y * (1.0 - lam_init))

    s_self = jnp.sum(qrow_s[...] * kn, axis=1, keepdims=True)
    sub_g = lax.broadcasted_iota(jnp.int32, (LANES, LANES), 0)
    lane_g = lax.broadcasted_iota(jnp.int32, (LANES, LANES), 1)
    for hh in range(H_MOBA):
        g_row = jnp.where(lane1 < nb, gate_s[hh:hh + 1, :], -jnp.inf)
        g_col = _row_to_col(g_row)
        beats = (g_col > g_row) | ((g_col == g_row) & (sub_g < lane_g))
        rank = jnp.sum(jnp.where(beats, 1.0, 0.0), axis=0, keepdims=True)
        sel = (rank < k_sel) & (lane1 < nb)
        m_h = mbm_s[hh:hh + 1, :]
        ss = s_self[8 + hh:9 + hh]
        top = jnp.maximum(jnp.max(jnp.where(sel, m_h, NEG), axis=1, keepdims=True), ss)
        w = jnp.where(sel, jnp.exp(m_h - top), 0.0)
        w_self = jnp.exp(ss - top)
        r0 = hh * HEAD_DIM
        num = (jnp.sum(mba_s[r0:r0 + HEAD_DIM, :] * w, axis=1, keepdims=True)
               + w_self * vnc[MOBA_COL + r0:MOBA_COL + r0 + HEAD_DIM])
        den = jnp.sum(w * mbl_s[hh:hh + 1, :], axis=1, keepdims=True) + w_self
        pieces.append(num / den)
    o_ref[0] = jnp.concatenate(pieces + [dense[FOX_COL:]], axis=0)


def _decode_mixing(q_new, k_new, v_new, logf_new, cache_kt, cache_vt, cache_ft, page_table, lam,
                   subln, u_excl, layer_idx):
    bsz = q_new.shape[0]
    n_pages = page_table.shape[1]
    assert n_pages % 2 == 0 and 2 * PAGE_SIZE == MOBA_BLOCK
    nb = n_pages // 2
    assert nb <= LANES
    lam_init = 0.8 - 0.6 * math.exp(-0.3 * layer_idx)
    mw = H_MOBA * HEAD_DIM
    l_ = layer_idx
    per_step = next(c for c in (4, 2, 1) if nb % c == 0)

    def page(blk, half):
        return lambda b, p, pt: (l_, pt[b, 2 * (nb - 1 - (p * per_step + blk)) + half], 0, 0)

    page_specs, page_args = [], []
    for blk in range(per_step):
        page_specs += [pl.BlockSpec((1, 1, QKV_WIDTH, PAGE_SIZE), page(blk, 0)),
                       pl.BlockSpec((1, 1, QKV_WIDTH, PAGE_SIZE), page(blk, 1)),
                       pl.BlockSpec((1, 1, QKV_WIDTH, PAGE_SIZE), page(blk, 0)),
                       pl.BlockSpec((1, 1, QKV_WIDTH, PAGE_SIZE), page(blk, 1)),
                       pl.BlockSpec((1, 1, 8, PAGE_SIZE), page(blk, 0)),
                       pl.BlockSpec((1, 1, 8, PAGE_SIZE), page(blk, 1))]
        page_args += [cache_kt, cache_kt, cache_vt, cache_vt, cache_ft, cache_ft]
    const = lambda b, p, pt: (0, 0)
    tok = lambda b, p, pt: (b, 0, 0)
    grid_spec = pltpu.PrefetchScalarGridSpec(
        num_scalar_prefetch=1,
        grid=(bsz, nb // per_step),
        in_specs=[
            pl.BlockSpec((4, HEAD_DIM), const),
            pl.BlockSpec((2 * HEAD_DIM, 1), const),
            pl.BlockSpec((1, 1, QKV_WIDTH), tok),
            pl.BlockSpec((1, QKV_WIDTH, 1), tok),
            pl.BlockSpec((1, 1, QKV_WIDTH), tok),
            pl.BlockSpec((1, QKV_WIDTH, 1), tok),
            pl.BlockSpec((1, 1, LANES), tok),
        ] + page_specs + [pl.BlockSpec((MOBA_BLOCK, MOBA_BLOCK), const)],
        out_specs=pl.BlockSpec((1, QKV_WIDTH, 1), tok),
        scratch_shapes=[
            pltpu.VMEM((DEC_ROWS, QKV_WIDTH), F32),
            pltpu.VMEM((DEC_ROWS, LANES), F32),
            pltpu.VMEM((DEC_ROWS, LANES), F32),
            pltpu.VMEM((QKV_WIDTH, LANES), F32),
            pltpu.VMEM((8, LANES), F32),
            pltpu.VMEM((8, LANES), F32),
            pltpu.VMEM((8, LANES), F32),
            pltpu.VMEM((8, LANES), F32),
            pltpu.VMEM((8, LANES), F32),
            pltpu.VMEM((mw, LANES), F32),
        ],
    )
    out = pl.pallas_call(
        functools.partial(_decode_kernel, nb=nb, per_step=per_step, k_sel=min(MOBA_TOPK, nb),
                          lam_init=lam_init),
        grid_spec=grid_spec,
        out_shape=jax.ShapeDtypeStruct((bsz, QKV_WIDTH, 1), F32),
        compiler_params=_cparams(("parallel", "arbitrary")),
        name="decode_mix",
    )(page_table, lam, subln.reshape(2 * HEAD_DIM, 1), q_new.reshape(bsz, 1, QKV_WIDTH),
      q_new.reshape(bsz, QKV_WIDTH, 1), k_new.reshape(bsz, 1, QKV_WIDTH),
      v_new.reshape(bsz, QKV_WIDTH, 1), logf_new.reshape(bsz, 1, LANES), *page_args, u_excl)
    return out.reshape(bsz, QKV_WIDTH)


def _rope_tables(pos):
    half = HEAD_DIM // 2
    inv = ROPE_THETA ** (-jnp.arange(half, dtype=F32) / half)
    ang = pos.astype(F32)[:, None] * inv[None, :]
    cos, sin = jnp.cos(ang), jnp.sin(ang)
    zero = jnp.zeros_like(sin)
    reps = ROPE_WIDTH // HEAD_DIM
    cos_t = jnp.tile(jnp.concatenate([cos, cos], axis=1), (1, reps))
    sin_a = jnp.tile(jnp.concatenate([-sin, zero], axis=1), (1, reps))
    sin_b = jnp.tile(jnp.concatenate([zero, sin], axis=1), (1, reps))
    return cos_t, sin_a, sin_b


def _tri(n, strict_lower):
    j = jnp.arange(n)[:, None]
    s = jnp.arange(n)[None, :]
    return ((j > s) if strict_lower else (j <= s)).astype(BF16)


def kernel(x_prompt, x_sample, cache_k, cache_v, cache_logf, cache_mem_k, cache_mem_v, page_table,
           mem_prompt, ffn1_norm, ffn1_w_up, ffn1_w_down, mix_norm, w_in, b_forget, diff_lambda,
           diff_subln, w_branch, w_gate, b_gate, w_out, xattn_norm, w_xq, w_xk, w_xv, w_xo,
           ffn2_norm, ffn2_w_up, ffn2_w_down, final_norm):
    bp, sp, d = x_prompt.shape
    bs, ss, _ = x_sample.shape
    assert ss == 1
    depth = ffn1_norm.shape[0]
    n_mem = mem_prompt.shape[1]
    n_pool = cache_k.shape[1]
    past_len = page_table.shape[1] * PAGE_SIZE

    pos_p = jnp.arange(sp, dtype=jnp.int32)
    pos_s = jnp.full((bs,), past_len, dtype=jnp.int32)
    rope_p = _rope_tables(pos_p)
    rope_s = _rope_tables(pos_s)
    u_excl = _tri(MOBA_BLOCK, True)
    u_incl = _tri(MOBA_BLOCK, False)

    cache_kt = jnp.transpose(cache_k, (0, 1, 3, 4, 2)).reshape(depth, n_pool, QKV_WIDTH, PAGE_SIZE)
    cache_vt = jnp.transpose(cache_v, (0, 1, 3, 4, 2)).reshape(depth, n_pool, QKV_WIDTH, PAGE_SIZE)
    cache_ft = jnp.pad(jnp.swapaxes(cache_logf, 2, 3), ((0, 0), (0, 0), (0, 8 - H_FOX), (0, 0)))

    h_p = x_prompt.reshape(bp * sp, d)
    h_s = x_sample.reshape(bs, d)
    mem2 = mem_prompt.reshape(bp * n_mem, d)
    outs = {k: [] for k in ("k_p", "v_p", "f_p", "mk_p", "mv_p", "k_s", "v_s", "f_s")}

    for l in range(depth):
        wu1 = ffn1_w_up[l].astype(BF16)
        wd1 = ffn1_w_down[l].astype(BF16)
        wu2 = ffn2_w_up[l].astype(BF16)
        wd2 = ffn2_w_down[l].astype(BF16)
        w_qkv = w_in[l][:, :3 * QKV_WIDTH].astype(BF16)
        w_f = jnp.pad(w_in[l][:, 3 * QKV_WIDTH:], ((0, 0), (0, LANES - H_FOX))).astype(BF16)
        b_f = jnp.pad(b_forget[l], (0, LANES - H_FOX)).reshape(1, LANES)
        wbr = w_branch[l].astype(BF16)
        wg = w_gate[l].astype(BF16)
        wo = w_out[l].astype(BF16)
        wxq = w_xq[l].astype(BF16)
        wxo = w_xo[l].astype(BF16)
        wxkv = jnp.concatenate([w_xk[l], w_xv[l]], axis=1).astype(BF16)
        last = l == depth - 1

        mem_kv = _matmul(mem2, wxkv)
        mem_k = mem_kv[:, :X_WIDTH].reshape(bp, n_mem, X_WIDTH)
        mem_v = mem_kv[:, X_WIDTH:].reshape(bp, n_mem, X_WIDTH)
        outs["mk_p"].append(mem_k.reshape(bp, n_mem, X_HEADS, HEAD_DIM))
        outs["mv_p"].append(mem_v.reshape(bp, n_mem, X_HEADS, HEAD_DIM))

        h_p = _ffn(h_p, ffn1_norm[l], wu1, wd1)
        q, k, v, lf = _in_proj(h_p, mix_norm[l], w_qkv, w_f, b_f, *rope_p)
        logf = lf[:, :H_FOX].reshape(bp, sp, H_FOX)
        outs["k_p"].append(k.reshape(bp, sp, N_SLOTS, HEAD_DIM))
        outs["v_p"].append(v.reshape(bp, sp, N_SLOTS, HEAD_DIM))
        outs["f_p"].append(logf)
        q3, k3, v3 = (a.reshape(bp, sp, QKV_WIDTH) for a in (q, k, v))
        f_cum = _cumsum_lanes(jnp.swapaxes(logf, 1, 2).reshape(bp * H_FOX, sp), u_incl)
        f_row = f_cum.reshape(bp, H_FOX // 2, 2, sp)
        f_col = jnp.swapaxes(f_row, 2, 3)
        o_a = _diff_attention(q3, k3, v3, diff_lambda[l], diff_subln[l], l)
        o_b = _moba_attention(q3, k3, v3)
        o_c = _fox_attention(q3, k3, v3, f_col, f_row)
        o_d = _stick_attention(q3, k3, v3, u_excl)
        flat = lambda o: o.reshape(bp * sp, o.shape[-1])
        h_p = _mix_out(h_p, mix_norm[l], [flat(o_a), flat(o_b), flat(o_c), flat(o_d)],
                       wbr, wg, b_gate[l], wo)
        h_p = _cross_attention(h_p.reshape(bp, sp, d), xattn_norm[l], wxq, mem_k, mem_v,
                               wxo).reshape(bp * sp, d)
        h_p = _ffn(h_p, ffn2_norm[l], wu2, wd2, final_norm if last else None)

        h_s = _ffn(h_s, ffn1_norm[l], wu1, wd1)
        q_s, k_s, v_s, lf_s = _in_proj(h_s, mix_norm[l], w_qkv, w_f, b_f, *rope_s)
        outs["k_s"].append(k_s.reshape(bs, 1, N_SLOTS, HEAD_DIM))
        outs["v_s"].append(v_s.reshape(bs, 1, N_SLOTS, HEAD_DIM))
        outs["f_s"].append(lf_s[:, :H_FOX].reshape(bs, 1, H_FOX))
        o_s = _decode_mixing(q_s, k_s, v_s, lf_s, cache_kt, cache_vt, cache_ft, page_table,
                             diff_lambda[l], diff_subln[l], u_excl, l)
        h_s = _mix_out(h_s, mix_norm[l],
                       [o_s[:, :MOBA_COL], o_s[:, MOBA_COL:FOX_COL], o_s[:, FOX_COL:STICK_COL],
                        o_s[:, STICK_COL:]], wbr, wg, b_gate[l], wo)
        h_s = _cross_attention(h_s.reshape(bs, 1, d), xattn_norm[l], wxq,
                               cache_mem_k[l].reshape(bs, n_mem, X_WIDTH),
                               cache_mem_v[l].reshape(bs, n_mem, X_WIDTH), wxo).reshape(bs, d)
        h_s = _ffn(h_s, ffn2_norm[l], wu2, wd2, final_norm if last else None)

    st = jnp.stack
    return (h_p.reshape(bp, sp, d), h_s.reshape(bs, 1, d), st(outs["k_p"]), st(outs["v_p"]),
            st(outs["f_p"]), st(outs["mk_p"]), st(outs["mv_p"]), st(outs["k_s"]), st(outs["v_s"]),
            st(outs["f_s"]))
```
